```python
import jax
import jax.numpy as jnp
from jax import lax
import numpy as np

D_MODEL = 2048
BATCH = 4
SEQ = 2048
DEPTH = 4

GRID_W = 64
CTX_LEN = 256
N_EVEN = (DEPTH + 1) // 2
N_ODD = DEPTH // 2
Q_BLOCK = 128
ROPE_THETA = 10000.0
NORM_EPS = 1e-6

N_FOURIER_GROUPS = 4
FOURIER_GROUP_W = D_MODEL // 16
FOURIER_W = N_FOURIER_GROUPS * FOURIER_GROUP_W
HEAD_DIM = 128
N_Q_HEADS = (D_MODEL - FOURIER_W) // HEAD_DIM
N_KV_HEADS = 4
Q_W = N_Q_HEADS * HEAD_DIM
KV_W = N_KV_HEADS * HEAD_DIM
EVEN_KV_START = FOURIER_W + Q_W
EVEN_IN_W = EVEN_KV_START + 2 * KV_W

CONV_W = D_MODEL // 4
CONV_K = 31
MLA_HEADS = 12
MLA_NOPE = 128
MLA_ROPE = 64
MLA_V = 128
Q_LORA = 512
KV_LORA = 512
ODD_KV_START = 2 * CONV_W + Q_LORA
ODD_IN_W = ODD_KV_START + KV_LORA + MLA_ROPE
ODD_OUT_IN = CONV_W + MLA_HEADS * MLA_V

D_FF = ((8 * D_MODEL // 3 + 127) // 128) * 128
FFN_K = 3

kernel_name = 'hybrid_fourier_gqa_conformer_mla_dit'


def rms_norm(x, gain=None):
    xf = x.astype(jnp.float32)
    y = xf * lax.rsqrt(jnp.mean(xf * xf, axis=-1, keepdims=True) + NORM_EPS)
    if gain is not None:
        y = y * gain.astype(jnp.float32)
    return y.astype(x.dtype)


def layer_norm(x, gain, bias):
    xf = x.astype(jnp.float32)
    mu = jnp.mean(xf, axis=-1, keepdims=True)
    var = jnp.mean(jnp.square(xf - mu), axis=-1, keepdims=True)
    y = (xf - mu) * lax.rsqrt(var + NORM_EPS) * gain.astype(jnp.float32) + bias.astype(jnp.float32)
    return y.astype(x.dtype)


def modulate(x, shift, scale):
    return rms_norm(x) * (1 + scale) + shift


def depthwise_conv(x, w, b):
    K, C = w.shape
    y = lax.conv_general_dilated(
        x, w[:, None, :].astype(x.dtype), window_strides=(1,), padding=[(K // 2, K // 2)],
        dimension_numbers=('NWC', 'WIO', 'NWC'), feature_group_count=C)
    return y + b.astype(x.dtype)


def axial_rope_tables(L, rot_dim):
    rows = L // GRID_W
    row = jnp.repeat(jnp.arange(rows, dtype=jnp.float32), GRID_W)
    col = jnp.tile(jnp.arange(GRID_W, dtype=jnp.float32), rows)
    n = rot_dim // 4
    inv = jnp.power(ROPE_THETA, -jnp.arange(n, dtype=jnp.float32) / n)
    ang = jnp.concatenate([row[:, None] * inv, col[:, None] * inv], axis=-1)
    return jnp.cos(ang), jnp.sin(ang)


def apply_rope(x, cos, sin):
    h = x.shape[-1] // 2
    x1, x2 = x[..., :h], x[..., h:]
    c = cos[None, :, None, :].astype(x.dtype)
    s = sin[None, :, None, :].astype(x.dtype)
    return jnp.concatenate([x1 * c - x2 * s, x2 * c + x1 * s], axis=-1)


def attention(q, k, v):
    B, L, Hk, G, dk = q.shape
    dv = v.shape[-1]
    nb = L // Q_BLOCK
    scale = dk ** -0.5
    qb = jnp.moveaxis(q.reshape(B, nb, Q_BLOCK, Hk, G, dk), 1, 0)

    def one_block(qblk):
        s = jnp.einsum('bqhgd,bkhd->bhgqk', qblk, k).astype(jnp.float32) * scale
        p = jax.nn.softmax(s, axis=-1).astype(v.dtype)
        return jnp.einsum('bhgqk,bkhd->bqhgd', p, v)

    ob = lax.map(one_block, qb)
    return jnp.moveaxis(ob, 0, 1).reshape(B, L, Hk, G, dv)


def split_heads(z, n_heads):
    B, L, _ = z.shape
    return z.reshape(B, L, n_heads, -1)


def group_queries(q):
    B, L, Hq, d = q.shape
    return q.reshape(B, L, N_KV_HEADS, Hq // N_KV_HEADS, d)


def fourier_mix(f):
    B, L, _ = f.shape
    fg = f.reshape(B, L, N_FOURIER_GROUPS, FOURIER_GROUP_W).astype(jnp.float32)
    out = jnp.fft.fftn(fg, axes=(1, 3), norm='ortho').real
    return out.reshape(B, L, FOURIER_W).astype(f.dtype)


def gqa_fourier_mixer(hl, hc, w_in, q_gain, k_gain, w_out, cos, sin, need_ctx_out):
    B, S, _ = hl.shape
    CL = hc.shape[1]
    zl = hl @ w_in
    f_l = zl[..., :FOURIER_W]
    q_l = apply_rope(rms_norm(split_heads(zl[..., FOURIER_W:EVEN_KV_START], N_Q_HEADS), q_gain), cos, sin)
    k_l = apply_rope(rms_norm(split_heads(zl[..., EVEN_KV_START:EVEN_KV_START + KV_W], N_KV_HEADS), k_gain), cos, sin)
    v_l = split_heads(zl[..., EVEN_KV_START + KV_W:], N_KV_HEADS)
    zc_kv = hc @ w_in[:, EVEN_KV_START:]
    k_c = rms_norm(split_heads(zc_kv[..., :KV_W], N_KV_HEADS), k_gain)
    v_c = split_heads(zc_kv[..., KV_W:], N_KV_HEADS)
    o_l = attention(group_queries(q_l),
                    jnp.concatenate([k_l, k_c], axis=1),
                    jnp.concatenate([v_l, v_c], axis=1))
    y_l = jnp.concatenate([fourier_mix(f_l), o_l.reshape(B, S, Q_W)], axis=-1) @ w_out
    if not need_ctx_out:
        return y_l, None
    zc = hc @ w_in[:, :EVEN_KV_START]
    q_c = rms_norm(split_heads(zc[..., FOURIER_W:], N_Q_HEADS), q_gain)
    o_c = attention(group_queries(q_c), k_c, v_c)
    y_c = jnp.concatenate([fourier_mix(zc[..., :FOURIER_W]), o_c.reshape(B, CL, Q_W)], axis=-1) @ w_out
    return y_l, y_c


def conformer_conv(glu_in, conv_w, conv_b, ln_g, ln_b):
    a, g = jnp.split(glu_in, 2, axis=-1)
    u = depthwise_conv(a * jax.nn.sigmoid(g), conv_w, conv_b)
    return jax.nn.silu(layer_norm(u, ln_g, ln_b))


def mla_kv(zkv, kv_norm, w_ukv, cos, sin):
    B, L, _ = zkv.shape
    kv = (rms_norm(zkv[..., :KV_LORA], kv_norm) @ w_ukv).reshape(B, L, MLA_HEADS, MLA_NOPE + MLA_V)
    k_nope, v = kv[..., :MLA_NOPE], kv[..., MLA_NOPE:]
    k_rope = zkv[..., KV_LORA:][:, :, None, :]
    if cos is not None:
        k_rope = apply_rope(k_rope, cos, sin)
    k = jnp.concatenate([k_nope, jnp.broadcast_to(k_rope, (B, L, MLA_HEADS, MLA_ROPE))], axis=-1)
    return k, v


def mla_q(cq, q_norm, w_uq, cos, sin):
    B, L, _ = cq.shape
    q = (rms_norm(cq, q_norm) @ w_uq).reshape(B, L, MLA_HEADS, MLA_NOPE + MLA_ROPE)
    if cos is not None:
        q = jnp.concatenate([q[..., :MLA_NOPE], apply_rope(q[..., MLA_NOPE:], cos, sin)], axis=-1)
    return q[:, :, :, None, :]


def conv_mla_mixer(hl, hc, w_in, conv_w, conv_b, ln_g, ln_b, q_norm, w_uq, kv_norm, w_ukv, w_out,
                   cos, sin, need_ctx_out):
    B, S, _ = hl.shape
    CL = hc.shape[1]
    zl = hl @ w_in
    k_l, v_l = mla_kv(zl[..., ODD_KV_START:], kv_norm, w_ukv, cos, sin)
    k_c, v_c = mla_kv(hc @ w_in[:, ODD_KV_START:], kv_norm, w_ukv, None, None)
    q_l = mla_q(zl[..., 2 * CONV_W:ODD_KV_START], q_norm, w_uq, cos, sin)
    o_l = attention(q_l, jnp.concatenate([k_l, k_c], axis=1), jnp.concatenate([v_l, v_c], axis=1))
    conv_l = conformer_conv(zl[..., :2 * CONV_W], conv_w, conv_b, ln_g, ln_b)
    y_l = jnp.concatenate([conv_l, o_l.reshape(B, S, MLA_HEADS * MLA_V)], axis=-1) @ w_out
    if not need_ctx_out:
        return y_l, None
    zc = hc @ w_in[:, :ODD_KV_START]
    o_c = attention(mla_q(zc[..., 2 * CONV_W:], q_norm, w_uq, None, None), k_c, v_c)
    conv_c = conformer_conv(zc[..., :2 * CONV_W], conv_w, conv_b, ln_g, ln_b)
    y_c = jnp.concatenate([conv_c, o_c.reshape(B, CL, MLA_HEADS * MLA_V)], axis=-1) @ w_out
    return y_l, y_c


def conv_glu_ffn(h, w_up, conv_w, conv_b, w_down):
    g, u = jnp.split(h @ w_up, 2, axis=-1)
    g = depthwise_conv(g, conv_w, conv_b)
    return (jax.nn.silu(g) * u) @ w_down


def setup_inputs(seed: int = 0) -> dict:
    key = jax.random.key(seed)
    keys = list(jax.random.split(key, 32))
    counter = [0]

    def nrm(shape, scale):
        k = keys[counter[0]]
        counter[0] += 1
        return scale * jax.random.normal(k, shape, jnp.float32)

    def gain(shape):
        return 1.0 + nrm(shape, 0.05)

    D = D_MODEL
    return {
        'x': nrm((BATCH, SEQ, D), 1.0),
        'c': nrm((BATCH, D), 1.0),
        'ctx': nrm((BATCH, CTX_LEN, D), 1.0),
        'c_ctx': nrm((D,), 1.0),
        'ada_w': nrm((DEPTH, D, 6 * D), 0.5 * D ** -0.5),
        'ada_b': nrm((DEPTH, 6 * D), 0.02),
        'ev_w_in': nrm((N_EVEN, D, EVEN_IN_W), D ** -0.5),
        'ev_q_gain': gain((N_EVEN, HEAD_DIM)),
        'ev_k_gain': gain((N_EVEN, HEAD_DIM)),
        'ev_w_out': nrm((N_EVEN, FOURIER_W + Q_W, D), (FOURIER_W + Q_W) ** -0.5),
        'od_w_in': nrm((N_ODD, D, ODD_IN_W), D ** -0.5),
        'od_conv_w': nrm((N_ODD, CONV_K, CONV_W), CONV_K ** -0.5),
        'od_conv_b': nrm((N_ODD, CONV_W), 0.02),
        'od_ln_g': gain((N_ODD, CONV_W)),
        'od_ln_b': nrm((N_ODD, CONV_W), 0.02),
        'od_q_norm': gain((N_ODD, Q_LORA)),
        'od_w_uq': nrm((N_ODD, Q_LORA, MLA_HEADS * (MLA_NOPE + MLA_ROPE)), Q_LORA ** -0.5),
        'od_kv_norm': gain((N_ODD, KV_LORA)),
        'od_w_ukv': nrm((N_ODD, KV_LORA, MLA_HEADS * (MLA_NOPE + MLA_V)), KV_LORA ** -0.5),
        'od_w_out': nrm((N_ODD, ODD_OUT_IN, D), ODD_OUT_IN ** -0.5),
        'ffn_w_up': nrm((DEPTH, D, 2 * D_FF), D ** -0.5),
        'ffn_conv_w': nrm((DEPTH, FFN_K, D_FF), FFN_K ** -0.5),
        'ffn_conv_b': nrm((DEPTH, D_FF), 0.02),
        'ffn_w_down': nrm((DEPTH, D_FF, D), D_FF ** -0.5),
        'final_norm': gain((D,)),
    }


def reference(x, c, ctx, c_ctx, ada_w, ada_b, ev_w_in, ev_q_gain, ev_k_gain, ev_w_out,
              od_w_in, od_conv_w, od_conv_b, od_ln_g, od_ln_b, od_q_norm, od_w_uq, od_kv_norm,
              od_w_ukv, od_w_out, ffn_w_up, ffn_conv_w, ffn_conv_b, ffn_w_down, final_norm):
    S = x.shape[1]
    cos_a, sin_a = axial_rope_tables(S, HEAD_DIM)
    cos_m, sin_m = axial_rope_tables(S, MLA_ROPE)
    silu_c = jax.nn.silu(c)
    silu_cc = jax.nn.silu(c_ctx)
    xl, xc = x, ctx
    for i in range(DEPTH):
        need_ctx = i < DEPTH - 1
        mod_l = (silu_c @ ada_w[i] + ada_b[i])[:, None, :]
        mod_c = silu_cc @ ada_w[i] + ada_b[i]
        sh1, sc1, g1, sh2, sc2, g2 = jnp.split(mod_l, 6, axis=-1)
        sh1c, sc1c, g1c, sh2c, sc2c, g2c = jnp.split(mod_c, 6, axis=-1)
        hl = modulate(xl, sh1, sc1)
        hc = modulate(xc, sh1c, sc1c)
        j = i // 2
        if i % 2 == 0:
            yl, yc = gqa_fourier_mixer(hl, hc, ev_w_in[j], ev_q_gain[j], ev_k_gain[j], ev_w_out[j],
                                       cos_a, sin_a, need_ctx)
        else:
            yl, yc = conv_mla_mixer(hl, hc, od_w_in[j], od_conv_w[j], od_conv_b[j], od_ln_g[j], od_ln_b[j],
                                    od_q_norm[j], od_w_uq[j], od_kv_norm[j], od_w_ukv[j], od_w_out[j],
                                    cos_m, sin_m, need_ctx)
        xl = xl + g1 * yl
        xl = xl + g2 * conv_glu_ffn(modulate(xl, sh2, sc2), ffn_w_up[i], ffn_conv_w[i], ffn_conv_b[i], ffn_w_down[i])
        if need_ctx:
            xc = xc + g1c * yc
            xc = xc + g2c * conv_glu_ffn(modulate(xc, sh2c, sc2c), ffn_w_up[i], ffn_conv_w[i], ffn_conv_b[i], ffn_w_down[i])
    return rms_norm(xl, final_norm)
```

```python
import functools

import jax
import jax.numpy as jnp
from jax import lax
from jax.experimental import pallas as pl
from jax.experimental.pallas import tpu as pltpu

F32 = jnp.float32
BF16 = jnp.bfloat16

D = 2048
NB = 4
S = 2048
DEPTH = 4
GRID_W = 64
CL = 256
ROPE_THETA = 10000.0
EPS = 1e-6

N_FG = 4
FG_W = 128
FOURIER_W = 512
HEAD_DIM = 128
N_Q_HEADS = 12
N_KV_HEADS = 4
GQA_G = N_Q_HEADS // N_KV_HEADS
Q_W = 1536
KV_W = 512
EVEN_IN_W = 3072

CONV_W = 512
CONV_K = 31
MLA_HEADS = 12
MLA_NOPE = 128
MLA_ROPE = 64
MLA_V = 128
Q_LORA = 512
KV_LORA = 512
MLA_QK = 256

D_FF = 5504
FFN_K = 3

ML = NB * S
MC = NB * CL
MT = ML + MC

LANE = 128
BF16_ROWS = 16
VMEM_LIMIT = 56 * 1024 * 1024


def _cparams(n_axes):
    return pltpu.CompilerParams(
        dimension_semantics=("arbitrary",) * n_axes, vmem_limit_bytes=VMEM_LIMIT)


def _seg(m, tm):
    return jnp.minimum((m * tm) // S, NB)


def _mod_spec(which, tm, m_axis):
    def idx(*g):
        return (_seg(g[m_axis], tm), which, 0, 0)
    return pl.BlockSpec((None, None, 1, D), idx)


def _rope_idx(m, tm):
    per_seq = S // tm
    n_lat = ML // tm
    return jnp.where(m < n_lat, m % per_seq, per_seq + (m - n_lat))


def _modulate(x, sh, sc):
    ms = jnp.mean(x * x, axis=-1, keepdims=True)
    return x * lax.rsqrt(ms + EPS) * (1.0 + sc) + sh


def _rope128(y, cos, sin):
    return y * cos + pltpu.roll(y, 64, 1) * sin


def _ada_kernel(c_ref, w_ref, b_ref, o_ref):
    c = c_ref[...]
    s = (c * jax.nn.sigmoid(c)).astype(BF16)
    o_ref[...] = jnp.dot(s, w_ref[...].astype(BF16), preferred_element_type=F32) + b_ref[...]


def _ada_call(cvec, ada_w, ada_b):
    tn = 1024
    return pl.pallas_call(
        _ada_kernel,
        grid=(DEPTH, 6 * D // tn),
        in_specs=[
            pl.BlockSpec((8, D), lambda l, j: (0, 0)),
            pl.BlockSpec((None, D, tn), lambda l, j: (l, 0, j)),
            pl.BlockSpec((None, 1, tn), lambda l, j: (l, 0, j)),
        ],
        out_specs=pl.BlockSpec((None, 8, tn), lambda l, j: (l, 0, j)),
        out_shape=jax.ShapeDtypeStruct((DEPTH, 8, 6 * D), F32),
        compiler_params=_cparams(2),
        name="ada_ln",
    )(cvec, ada_w, ada_b.reshape(DEPTH, 1, 6 * D))


def _init_kernel(x_ref, c_ref, sh_ref, sc_ref, xo_ref, h_ref, *, n_lat):
    m = pl.program_id(0)

    @pl.when(m < n_lat)
    def _():
        xo_ref[...] = x_ref[...]

    @pl.when(m >= n_lat)
    def _():
        xo_ref[...] = c_ref[...]

    h_ref[...] = _modulate(xo_ref[...], sh_ref[...], sc_ref[...]).astype(BF16)


def _init_call(x2, c2, mod):
    tm = 512
    n_lat = ML // tm
    return pl.pallas_call(
        functools.partial(_init_kernel, n_lat=n_lat),
        grid=(MT // tm,),
        in_specs=[
            pl.BlockSpec((tm, D), lambda m: (jnp.minimum(m, n_lat - 1), 0)),
            pl.BlockSpec((tm, D), lambda m: (jnp.maximum(m - n_lat, 0), 0)),
            _mod_spec(0, tm, 0),
            _mod_spec(1, tm, 0),
        ],
        out_specs=[pl.BlockSpec((tm, D), lambda m: (m, 0)),
                   pl.BlockSpec((tm, D), lambda m: (m, 0))],
        out_shape=[jax.ShapeDtypeStruct((MT, D), F32), jax.ShapeDtypeStruct((MT, D), BF16)],
        compiler_params=_cparams(1),
        name="init_modulate",
    )(x2, c2, mod, mod)


def _cast_weight(w_ref, wb_ref):
    @pl.when(pl.program_id(1) == 0)
    def _():
        wb_ref[...] = w_ref[...].astype(BF16)


def _even_inproj_kernel(x_ref, w_ref, gain_ref, cos_ref, sin_ref, o_ref, wb_ref, *, tn):
    n = pl.program_id(0)
    _cast_weight(w_ref, wb_ref)
    acc = jnp.dot(x_ref[...], wb_ref[...], preferred_element_type=F32)
    plain = jnp.logical_or(n == 0, n == 5)

    @pl.when(plain)
    def _():
        o_ref[...] = acc.astype(BF16)

    @pl.when(jnp.logical_not(plain))
    def _():
        for j in range(tn // HEAD_DIM):
            sl = slice(j * HEAD_DIM, (j + 1) * HEAD_DIM)
            a = acc[:, sl]
            y = a * lax.rsqrt(jnp.mean(a * a, axis=-1, keepdims=True) + EPS) * gain_ref[:, sl]
            o_ref[:, sl] = _rope128(y, cos_ref[...], sin_ref[...]).astype(BF16)


def _even_inproj_call(h, w, gain_full, cos_t, sin_t):
    tm, tn = 1024, 512
    def out_idx(n, m):
        return (m, jnp.where(n == 0, 3, jnp.where(n < 4, n - 1, n)))
    return pl.pallas_call(
        functools.partial(_even_inproj_kernel, tn=tn),
        grid=(EVEN_IN_W // tn, MT // tm),
        in_specs=[
            pl.BlockSpec((tm, D), lambda n, m: (m, 0)),
            pl.BlockSpec((D, tn), lambda n, m: (0, n)),
            pl.BlockSpec((1, tn), lambda n, m: (0, n)),
            pl.BlockSpec((tm, HEAD_DIM), lambda n, m: (_rope_idx(m, tm), 0)),
            pl.BlockSpec((tm, HEAD_DIM), lambda n, m: (_rope_idx(m, tm), 0)),
        ],
        out_specs=pl.BlockSpec((tm, tn), out_idx),
        out_shape=jax.ShapeDtypeStruct((MT, EVEN_IN_W), BF16),
        scratch_shapes=[pltpu.VMEM((D, tn), BF16)],
        compiler_params=_cparams(2),
        name="even_in_proj",
    )(h, w, gain_full, cos_t, sin_t)


def _plain_proj_kernel(x_ref, w_ref, o_ref, wb_ref):
    _cast_weight(w_ref, wb_ref)
    o_ref[...] = jnp.dot(x_ref[...], wb_ref[...], preferred_element_type=F32).astype(BF16)


def _odd_inproj_call(h, w):
    tm, tn, n_out = 1024, 512, 2048
    return pl.pallas_call(
        _plain_proj_kernel,
        grid=(n_out // tn, MT // tm),
        in_specs=[
            pl.BlockSpec((tm, D), lambda n, m: (m, 0)),
            pl.BlockSpec((D, tn), lambda n, m: (0, n)),
        ],
        out_specs=pl.BlockSpec((tm, tn), lambda n, m: (m, n)),
        out_shape=jax.ShapeDtypeStruct((MT, n_out), BF16),
        scratch_shapes=[pltpu.VMEM((D, tn), BF16)],
        compiler_params=_cparams(2),
        name="odd_in_proj",
    )(h, w)


def _small_proj_kernel(x_ref, g_ref, w_ref, cos_ref, sin_ref, o_ref, wb_ref, *, norm, rope, cw):
    @pl.when(pl.program_id(0) == 0)
    def _():
        wb_ref[...] = w_ref[...].astype(BF16)

    x = x_ref[...]
    if norm:
        xf = x.astype(F32)
        x = (xf * lax.rsqrt(jnp.mean(xf * xf, axis=-1, keepdims=True) + EPS) * g_ref[...]).astype(BF16)
    n_out = o_ref.shape[1]
    for j in range(n_out // cw):
        acc = jnp.dot(x, wb_ref[:, j * cw:(j + 1) * cw], preferred_element_type=F32)
        for i in range(cw // LANE):
            col = j * cw + i * LANE
            a = acc[:, i * LANE:(i + 1) * LANE]
            roped = rope == "all" or (rope == "odd" and (col // LANE) % 2 == 1)
            if roped:
                a = _rope128(a, cos_ref[...], sin_ref[...])
            o_ref[:, col:col + LANE] = a.astype(BF16)


def _small_proj_call(x, x_col_block, k_in, gain, w, cos_t, sin_t, *, n_rows, norm, rope, name):
    tm = 512
    n_out = w.shape[1]
    cw = min(512, n_out)
    return pl.pallas_call(
        functools.partial(_small_proj_kernel, norm=norm, rope=rope, cw=cw),
        grid=(n_rows // tm,),
        in_specs=[
            pl.BlockSpec((tm, k_in), lambda m: (m, x_col_block)),
            pl.BlockSpec((1, k_in), lambda m: (0, 0)),
            pl.BlockSpec((k_in, n_out), lambda m: (0, 0)),
            pl.BlockSpec((tm, LANE), lambda m: (_rope_idx(m, tm), 0)),
            pl.BlockSpec((tm, LANE), lambda m: (_rope_idx(m, tm), 0)),
        ],
        out_specs=pl.BlockSpec((tm, n_out), lambda m: (m, 0)),
        out_shape=jax.ShapeDtypeStruct((n_rows, n_out), BF16),
        scratch_shapes=[pltpu.VMEM((k_in, n_out), BF16)],
        compiler_params=_cparams(1),
        name=name,
    )(x, gain, w, cos_t, sin_t)


def _attn_kernel(*refs, n_kparts, has_lat, groups, dk, dv, scale):
    q_ref = refs[0]
    pos = 1
    klat = kctx = ()
    if has_lat:
        klat = refs[pos:pos + n_kparts]
        pos += n_kparts
    kctx = refs[pos:pos + n_kparts]
    pos += n_kparts
    vlat_ref = None
    if has_lat:
        vlat_ref = refs[pos]
        pos += 1
    vctx_ref = refs[pos]
    o_ref = refs[-1]

    def keys(parts):
        if len(parts) == 1:
            return parts[0][...]
        return jnp.concatenate([p[...] for p in parts], axis=1)

    dn = (((1,), (1,)), ((), ()))
    kc = keys(kctx)
    kl = keys(klat) if has_lat else None
    for g in range(groups):
        q = q_ref[:, g * dk:(g + 1) * dk]
        sc = lax.dot_general(q, kc, dn, preferred_element_type=F32) * scale
        mx = jnp.max(sc, axis=-1, keepdims=True)
        if has_lat:
            sl = lax.dot_general(q, kl, dn, preferred_element_type=F32) * scale
            mx = jnp.maximum(mx, jnp.max(sl, axis=-1, keepdims=True))
        pc = jnp.exp(sc - mx)
        den = jnp.sum(pc, axis=-1, keepdims=True)
        o = jnp.dot(pc.astype(BF16), vctx_ref[...], preferred_element_type=F32)
        if has_lat:
            pl_ = jnp.exp(sl - mx)
            den = den + jnp.sum(pl_, axis=-1, keepdims=True)
            o = o + jnp.dot(pl_.astype(BF16), vlat_ref[...], preferred_element_type=F32)
        o_ref[:, g * dv:(g + 1) * dv] = (o / den).astype(BF16)


def _attn_calls(q_arr, q_blk0, k_arrs, k_blk_fns, v_arr, v_blk_fn, *, n_heads, groups, dk, dv,
                scale, need_ctx, name):
    tq = 512
    n_kparts = len(k_arrs)
    kern = functools.partial(_attn_kernel, n_kparts=n_kparts, groups=groups, dk=dk, dv=dv, scale=scale)
    qw, ow = groups * dk, groups * dv
    ctx_blk0 = ML // CL

    lat_k_specs = [pl.BlockSpec((S, LANE), (lambda b, h, i, f=f: (b, f(h)))) for f in k_blk_fns]
    ctx_k_specs = [pl.BlockSpec((CL, LANE), (lambda b, h, i, f=f: (ctx_blk0 + b, f(h)))) for f in k_blk_fns]
    out = pl.pallas_call(
        functools.partial(kern, has_lat=True),
        grid=(NB, n_heads, S // tq),
        in_specs=[pl.BlockSpec((tq, qw), lambda b, h, i: (b * (S // tq) + i, q_blk0 + h))]
        + lat_k_specs + ctx_k_specs
        + [pl.BlockSpec((S, dv), lambda b, h, i: (b, v_blk_fn(h))),
           pl.BlockSpec((CL, dv), lambda b, h, i: (ctx_blk0 + b, v_blk_fn(h)))],
        out_specs=pl.BlockSpec((tq, ow), lambda b, h, i: (b * (S // tq) + i, h)),
        out_shape=jax.ShapeDtypeStruct((MT if need_ctx else ML, n_heads * ow), BF16),
        compiler_params=_cparams(3),
        name=name + "_lat",
    )(q_arr, *k_arrs, *k_arrs, v_arr, v_arr)
    if not need_ctx:
        return out
    ctx_k_specs2 = [pl.BlockSpec((CL, LANE), (lambda b, h, f=f: (ctx_blk0 + b, f(h)))) for f in k_blk_fns]
    n_in = 1 + n_kparts + 1
    return pl.pallas_call(
        functools.partial(kern, has_lat=False),
        grid=(NB, n_heads),
        in_specs=[pl.BlockSpec((CL, qw), lambda b, h: (ctx_blk0 + b, q_blk0 + h))]
        + ctx_k_specs2
        + [pl.BlockSpec((CL, dv), lambda b, h: (ctx_blk0 + b, v_blk_fn(h))),
           pl.BlockSpec(memory_space=pl.ANY)],
        out_specs=pl.BlockSpec((CL, ow), lambda b, h: (ctx_blk0 + b, h)),
        out_shape=jax.ShapeDtypeStruct((MT, n_heads * ow), BF16),
        input_output_aliases={n_in: 0},
        compiler_params=_cparams(2),
        name=name + "_ctx",
    )(q_arr, *k_arrs, v_arr, out)


def _fourier_kernel(x_ref, wc_ref, cl_ref, sl_ref, *rest):
    o_ref, ya_ref, yb_ref = rest[-3:]

    @pl.when(pl.program_id(1) == 0)
    def _():
        for g in range(N_FG):
            sl = slice(g * FG_W, (g + 1) * FG_W)
            y = jnp.dot(x_ref[:, sl], wc_ref[...], preferred_element_type=F32)
            ya_ref[:, sl] = y[:, :FG_W].astype(BF16)
            yb_ref[:, sl] = y[:, FG_W:].astype(BF16)

    o = (jnp.dot(cl_ref[...], ya_ref[...], preferred_element_type=F32)
         - jnp.dot(sl_ref[...], yb_ref[...], preferred_element_type=F32))
    o_ref[...] = o.astype(BF16)


def _dft_mats(n):
    idx = jnp.arange(n, dtype=jnp.int32)
    k = (idx[:, None] * idx[None, :]) % n
    ang = k.astype(F32) * (2.0 * jnp.pi / n)
    nrm = float(n) ** -0.5
    return jnp.cos(ang) * nrm, jnp.sin(ang) * nrm


def _fourier_calls(z, f_col_block, need_ctx):
    cc, sc = _dft_mats(FG_W)
    wc = jnp.concatenate([cc, sc], axis=1).astype(BF16)
    cl, sl = _dft_mats(S)
    cl, sl = cl.astype(BF16), sl.astype(BF16)
    tr = 512
    out = pl.pallas_call(
        _fourier_kernel,
        grid=(NB, S // tr),
        in_specs=[
            pl.BlockSpec((S, FOURIER_W), lambda b, r: (b, f_col_block)),
            pl.BlockSpec((FG_W, 2 * FG_W), lambda b, r: (0, 0)),
            pl.BlockSpec((tr, S), lambda b, r: (r, 0)),
            pl.BlockSpec((tr, S), lambda b, r: (r, 0)),
        ],
        out_specs=pl.BlockSpec((tr, FOURIER_W), lambda b, r: (b * (S // tr) + r, 0)),
        out_shape=jax.ShapeDtypeStruct((MT, FOURIER_W), BF16),
        scratch_shapes=[pltpu.VMEM((S, FOURIER_W), BF16), pltpu.VMEM((S, FOURIER_W), BF16)],
        compiler_params=_cparams(2),
        name="fourier_lat",
    )(z, wc, cl, sl)
    if not need_ctx:
        return out
    clc, slc = _dft_mats(CL)
    clc, slc = clc.astype(BF16), slc.astype(BF16)
    ctx_blk0 = ML // CL
    return pl.pallas_call(
        _fourier_kernel,
        grid=(NB, 1),
        in_specs=[
            pl.BlockSpec((CL, FOURIER_W), lambda b, r: (ctx_blk0 + b, f_col_block)),
            pl.BlockSpec((FG_W, 2 * FG_W), lambda b, r: (0, 0)),
            pl.BlockSpec((CL, CL), lambda b, r: (0, 0)),
            pl.BlockSpec((CL, CL), lambda b, r: (0, 0)),
            pl.BlockSpec(memory_space=pl.ANY),
        ],
        out_specs=pl.BlockSpec((CL, FOURIER_W), lambda b, r: (ctx_blk0 + b, 0)),
        out_shape=jax.ShapeDtypeStruct((MT, FOURIER_W), BF16),
        scratch_shapes=[pltpu.VMEM((CL, FOURIER_W), BF16), pltpu.VMEM((CL, FOURIER_W), BF16)],
        input_output_aliases={4: 0},
        compiler_params=_cparams(2),
        name="fourier_ctx",
    )(z, wc, clc, slc, out)


CONV_TM = 256
CONV_HALO = 16
CONV_CHUNK = 32


def _conv_kernel(a_ref, g_ref, ap_ref, gp_ref, an_ref, gn_ref, w_ref, b_ref, lg_ref, lb_ref,
                 o_ref, ext_ref, *, n_lat, per_seq):
    m = pl.program_id(0)
    is_ctx = m >= n_lat
    first = jnp.logical_or(is_ctx, m % per_seq == 0)
    last = jnp.logical_or(is_ctx, m % per_seq == per_seq - 1)

    def glu(a, g):
        return a.astype(F32) * jax.nn.sigmoid(g.astype(F32))

    tm, hl = CONV_TM, CONV_HALO
    ext_ref[0:hl, :] = jnp.where(first, 0.0, glu(ap_ref[...], gp_ref[...]))
    ext_ref[hl:hl + tm, :] = glu(a_ref[...], g_ref[...])
    ext_ref[hl + tm:, :] = jnp.where(last, 0.0, glu(an_ref[...], gn_ref[...]))

    pad = CONV_K // 2
    for c in range(tm // CONV_CHUNK):
        base = c * CONV_CHUNK + hl - pad
        acc = jnp.zeros((CONV_CHUNK, CONV_W), F32) + b_ref[...]
        for k in range(CONV_K):
            acc = acc + w_ref[k:k + 1, :] * ext_ref[base + k:base + k + CONV_CHUNK, :]
        mu = jnp.mean(acc, axis=-1, keepdims=True)
        d = acc - mu
        var = jnp.mean(d * d, axis=-1, keepdims=True)
        y = d * lax.rsqrt(var + EPS) * lg_ref[...] + lb_ref[...]
        o_ref[c * CONV_CHUNK:(c + 1) * CONV_CHUNK, :] = (y * jax.nn.sigmoid(y)).astype(BF16)


def _conv_call(z, conv_w, conv_b, ln_g, ln_b, n_rows):
    tm, hl = CONV_TM, CONV_HALO
    r = tm // hl
    n_halo_blocks = z.shape[0] // hl
    prev = lambda m: jnp.maximum(m * r - 1, 0)
    nxt = lambda m: jnp.minimum((m + 1) * r, n_halo_blocks - 1)
    row = lambda v: v.reshape(1, CONV_W)
    return pl.pallas_call(
        functools.partial(_conv_kernel, n_lat=ML // tm, per_seq=S // tm),
        grid=(n_rows // tm,),
        in_specs=[
            pl.BlockSpec((tm, CONV_W), lambda m: (m, 0)),
            pl.BlockSpec((tm, CONV_W), lambda m: (m, 1)),
            pl.BlockSpec((hl, CONV_W), lambda m: (prev(m), 0)),
            pl.BlockSpec((hl, CONV_W), lambda m: (prev(m), 1)),
            pl.BlockSpec((hl, CONV_W), lambda m: (nxt(m), 0)),
            pl.BlockSpec((hl, CONV_W), lambda m: (nxt(m), 1)),
            pl.BlockSpec((CONV_K, CONV_W), lambda m: (0, 0)),
            pl.BlockSpec((1, CONV_W), lambda m: (0, 0)),
            pl.BlockSpec((1, CONV_W), lambda m: (0, 0)),
            pl.BlockSpec((1, CONV_W), lambda m: (0, 0)),
        ],
        out_specs=pl.BlockSpec((tm, CONV_W), lambda m: (m, 0)),
        out_shape=jax.ShapeDtypeStruct((n_rows, CONV_W), BF16),
        scratch_shapes=[pltpu.VMEM((tm + 2 * hl, CONV_W), F32)],
        compiler_params=_cparams(1),
        name="conformer_conv",
    )(z, z, z, z, z, z, conv_w, row(conv_b), row(ln_g), row(ln_b))


def _resproj_kernel(*refs, ks, kc, wc, final):
    n = len(ks)
    lhs = refs[:n]
    w_hbm, x_ref, gate_ref = refs[n:n + 3]
    if final:
        gain_ref, o_ref = refs[n + 3:n + 5]
        wb_ref, stage_ref, sem = refs[n + 5:]
    else:
        sh_ref, sc_ref, xo_ref, h_ref = refs[n + 3:n + 7]
        wb_ref, stage_ref, sem = refs[n + 7:]
    k_total = sum(ks)

    @pl.when(pl.program_id(0) == 0)
    def _():
        chunks = [(r, c) for r in range(k_total // kc) for c in range(D // wc)]

        def copy(i):
            r, c = chunks[i]
            return pltpu.make_async_copy(
                w_hbm.at[pl.ds(r * kc, kc), pl.ds(c * wc, wc)], stage_ref.at[i % 2], sem.at[i % 2])

        copy(0).start()
        for i, (r, c) in enumerate(chunks):
            if i + 1 < len(chunks):
                copy(i + 1).start()
            copy(i).wait()
            wb_ref[r * kc:(r + 1) * kc, c * wc:(c + 1) * wc] = stage_ref[i % 2].astype(BF16)

    off = 0
    acc = None
    for ref, k in zip(lhs, ks):
        part = jnp.dot(ref[...], wb_ref[off:off + k, :], preferred_element_type=F32)
        acc = part if acc is None else acc + part
        off += k
    xn = x_ref[...] + gate_ref[...] * acc
    if final:
        ms = jnp.mean(xn * xn, axis=-1, keepdims=True)
        o_ref[...] = xn * lax.rsqrt(ms + EPS) * gain_ref[...]
    else:
        xo_ref[...] = xn
        h_ref[...] = _modulate(xn, sh_ref[...], sc_ref[...]).astype(BF16)


def _resproj_call(lhs, w, x, mod, gate_idx, next_mod, next_idx, final_gain, *, n_rows, tm, kc, wc, name):
    ks = tuple(a.shape[1] for a in lhs)
    k_total = sum(ks)
    final = final_gain is not None
    in_specs = [pl.BlockSpec((tm, k), lambda m: (m, 0)) for k in ks]
    in_specs += [pl.BlockSpec(memory_space=pl.ANY),
                 pl.BlockSpec((tm, D), lambda m: (m, 0)),
                 _mod_spec(gate_idx, tm, 0)]
    args = list(lhs) + [w, x, mod]
    if final:
        in_specs.append(pl.BlockSpec((1, D), lambda m: (0, 0)))
        args.append(final_gain.reshape(1, D))
        out_specs = pl.BlockSpec((tm, D), lambda m: (m, 0))
        out_shape = jax.ShapeDtypeStruct((n_rows, D), F32)
    else:
        in_specs += [_mod_spec(next_idx[0], tm, 0), _mod_spec(next_idx[1], tm, 0)]
        args += [next_mod, next_mod]
        out_specs = [pl.BlockSpec((tm, D), lambda m: (m, 0)), pl.BlockSpec((tm, D), lambda m: (m, 0))]
        out_shape = [jax.ShapeDtypeStruct((n_rows, D), F32), jax.ShapeDtypeStruct((n_rows, D), BF16)]
    return pl.pallas_call(
        functools.partial(_resproj_kernel, ks=ks, kc=kc, wc=wc, final=final),
        grid=(n_rows // tm,),
        in_specs=in_specs,
        out_specs=out_specs,
        out_shape=out_shape,
        scratch_shapes=[pltpu.VMEM((k_total, D), BF16),
                        pltpu.VMEM((2, kc, wc), F32),
                        pltpu.SemaphoreType.DMA((2,))],
        compiler_params=_cparams(1),
        name=name,
    )(*args)


FFN_HALO = 16


def _ffn_up_kernel(x_ref, xp_ref, xn_ref, wg_ref, wu0_ref, wu1_ref, cw_ref, cb_ref, o_ref,
                   wgb_ref, wub_ref, xe_ref, *, tm, tn, u_skip):
    n = pl.program_id(0)
    m = pl.program_id(1)

    @pl.when(m == 0)
    def _():
        wgb_ref[...] = wg_ref[...].astype(BF16)
        wub_ref[:, :tn - u_skip] = wu0_ref[:, u_skip:].astype(BF16)
        wub_ref[:, tn - u_skip:] = wu1_ref[:, :u_skip].astype(BF16)

    hl = FFN_HALO
    xe_ref[0:hl, :] = xp_ref[...]
    xe_ref[hl:hl + tm, :] = x_ref[...]
    xe_ref[hl + tm:, :] = xn_ref[...]
    ge = jnp.dot(xe_ref[...], wgb_ref[...], preferred_element_type=F32)
    u = jnp.dot(x_ref[...], wub_ref[...], preferred_element_type=F32)
    rows = tm + 2 * hl
    g_prev = pltpu.roll(ge, 1, 0)[hl:hl + tm]
    g_next = pltpu.roll(ge, rows - 1, 0)[hl:hl + tm]
    g_mid = ge[hl:hl + tm]
    r = m * tm + lax.broadcasted_iota(jnp.int32, (tm, tn), 0)
    seq = jnp.where(r < ML, S, CL)
    t = jnp.bitwise_and(r, seq - 1)
    g_prev = jnp.where(t == 0, 0.0, g_prev)
    g_next = jnp.where(t == seq - 1, 0.0, g_next)
    conv = cw_ref[0:1, :] * g_prev + cw_ref[1:2, :] * g_mid + cw_ref[2:3, :] * g_next + cb_ref[...]
    val = conv * jax.nn.sigmoid(conv) * u
    col = n * tn + lax.broadcasted_iota(jnp.int32, (tm, tn), 1)
    o_ref[...] = jnp.where(col < D_FF, val, 0.0).astype(BF16)


def _ffn_up_call(h, w_up, conv_w, conv_b, n_rows):
    tm, tn, hl = 1024, 512, FFN_HALO
    r = tm // hl
    n_halo_blocks = n_rows // hl
    u_blk0, u_skip = divmod(D_FF, tn)
    last_blk = pl.cdiv(2 * D_FF, tn) - 1
    return pl.pallas_call(
        functools.partial(_ffn_up_kernel, tm=tm, tn=tn, u_skip=u_skip),
        grid=(pl.cdiv(D_FF, tn), n_rows // tm),
        in_specs=[
            pl.BlockSpec((tm, D), lambda n, m: (m, 0)),
            pl.BlockSpec((hl, D), lambda n, m: (jnp.maximum(m * r - 1, 0), 0)),
            pl.BlockSpec((hl, D), lambda n, m: (jnp.minimum((m + 1) * r, n_halo_blocks - 1), 0)),
            pl.BlockSpec((D, tn), lambda n, m: (0, n)),
            pl.BlockSpec((D, tn), lambda n, m: (0, u_blk0 + n)),
            pl.BlockSpec((D, tn), lambda n, m: (0, jnp.minimum(u_blk0 + n + 1, last_blk))),
            pl.BlockSpec((FFN_K, tn), lambda n, m: (0, n)),
            pl.BlockSpec((1, tn), lambda n, m: (0, n)),
        ],
        out_specs=pl.BlockSpec((tm, tn), lambda n, m: (m, n)),
        out_shape=jax.ShapeDtypeStruct((n_rows, D_FF), BF16),
        scratch_shapes=[pltpu.VMEM((D, tn), BF16), pltpu.VMEM((D, tn), BF16),
                        pltpu.VMEM((tm + 2 * hl, D), BF16)],
        compiler_params=_cparams(2),
        name="ffn_up",
    )(h, h, h, w_up, w_up, w_up, conv_w, conv_b.reshape(1, D_FF))


def _axial_rope_tables(length, rot_dim):
    rows = length // GRID_W
    row = jnp.repeat(jnp.arange(rows, dtype=F32), GRID_W)
    col = jnp.tile(jnp.arange(GRID_W, dtype=F32), rows)
    n = rot_dim // 4
    inv = jnp.power(ROPE_THETA, -jnp.arange(n, dtype=F32) / n)
    ang = jnp.concatenate([row[:, None] * inv, col[:, None] * inv], axis=-1)
    return jnp.cos(ang), jnp.sin(ang)


def _rope_tables_gqa():
    c, s = _axial_rope_tables(S, HEAD_DIM)
    cos = jnp.concatenate([c, c], axis=1)
    sin = jnp.concatenate([-s, s], axis=1)
    ident_c = jnp.ones((MC, LANE), F32)
    ident_s = jnp.zeros((MC, LANE), F32)
    return jnp.concatenate([cos, ident_c], axis=0), jnp.concatenate([sin, ident_s], axis=0)


def _rope_tables_mla():
    c, s = _axial_rope_tables(S, MLA_ROPE)
    z = jnp.zeros_like(c)
    cos = jnp.concatenate([c, z, c, z], axis=1)
    sin = jnp.concatenate([-s, z, s, z], axis=1)
    one = jnp.ones((MC, 32), F32)
    zc = jnp.zeros((MC, 32), F32)
    ident_c = jnp.concatenate([one, zc, one, zc], axis=1)
    return jnp.concatenate([cos, ident_c], axis=0), jnp.concatenate([sin, jnp.zeros((MC, LANE), F32)], axis=0)


def _spread_rope_cols(w):
    h = MLA_ROPE // 2
    z = jnp.zeros(w.shape[:-1] + (h,), w.dtype)
    return jnp.concatenate([w[..., :h], z, w[..., h:], z], axis=-1)


def kernel(x, c, ctx, c_ctx, ada_w, ada_b, ev_w_in, ev_q_gain, ev_k_gain, ev_w_out, od_w_in, od_conv_w, od_conv_b, od_ln_g, od_ln_b, od_q_norm, od_w_uq, od_kv_norm, od_w_ukv, od_w_out, ffn_w_up, ffn_conv_w, ffn_conv_b, ffn_w_down, final_norm):
    cvec = jnp.concatenate([c, c_ctx[None, :], jnp.zeros((8 - NB - 1, D), F32)], axis=0)
    mod_all = _ada_call(cvec, ada_w, ada_b).reshape(DEPTH, 8, 6, 1, D)
    cos_a, sin_a = _rope_tables_gqa()
    cos_m, sin_m = _rope_tables_mla()

    xl, h = _init_call(x.reshape(ML, D), ctx.reshape(MC, D), mod_all[0])

    for i in range(DEPTH):
        mod = mod_all[i]
        need_ctx = i < DEPTH - 1
        n_rows = MT if need_ctx else ML
        j = i // 2
        if i % 2 == 0:
            gain_full = jnp.concatenate([
                jnp.ones((FOURIER_W,), F32), jnp.tile(ev_q_gain[j], N_Q_HEADS),
                jnp.tile(ev_k_gain[j], N_KV_HEADS), jnp.ones((KV_W,), F32)]).reshape(1, EVEN_IN_W)
            z = _even_inproj_call(h, ev_w_in[j], gain_full, cos_a, sin_a)
            att = _attn_calls(
                z, 0, [z], [lambda hk: (Q_W + FOURIER_W) // LANE + hk], z,
                lambda hk: (Q_W + FOURIER_W + KV_W) // LANE + hk,
                n_heads=N_KV_HEADS, groups=GQA_G, dk=HEAD_DIM, dv=HEAD_DIM,
                scale=HEAD_DIM ** -0.5, need_ctx=need_ctx, name="gqa")
            mix = _fourier_calls(z, Q_W // FOURIER_W, need_ctx)
            w_out = ev_w_out[j]
        else:
            w_in = od_w_in[j]
            z = _odd_inproj_call(h, w_in)
            w_kr = _spread_rope_cols(w_in[:, 2 * CONV_W + Q_LORA + KV_LORA:])
            ones = jnp.ones((1, D), F32)
            kr = _small_proj_call(h, 0, D, ones, w_kr, cos_m, sin_m,
                                  n_rows=MT, norm=False, rope="all", name="mla_k_rope")
            wq = od_w_uq[j].reshape(Q_LORA, MLA_HEADS, MLA_NOPE + MLA_ROPE)
            wq = jnp.concatenate([wq[..., :MLA_NOPE], _spread_rope_cols(wq[..., MLA_NOPE:])], axis=-1)
            q = _small_proj_call(z, 2, Q_LORA, od_q_norm[j].reshape(1, Q_LORA),
                                 wq.reshape(Q_LORA, MLA_HEADS * MLA_QK), cos_m, sin_m,
                                 n_rows=n_rows, norm=True, rope="odd", name="mla_q_up")
            kv = _small_proj_call(z, 3, KV_LORA, od_kv_norm[j].reshape(1, KV_LORA),
                                  od_w_ukv[j], cos_m, sin_m,
                                  n_rows=MT, norm=True, rope="none", name="mla_kv_up")
            att = _attn_calls(
                q, 0, [kv, kr], [lambda hh: 2 * hh, lambda hh: 0], kv, lambda hh: 2 * hh + 1,
                n_heads=MLA_HEADS, groups=1, dk=MLA_QK, dv=MLA_V,
                scale=(MLA_NOPE + MLA_ROPE) ** -0.5, need_ctx=need_ctx, name="mla")
            mix = _conv_call(z, od_conv_w[j], od_conv_b[j], od_ln_g[j], od_ln_b[j], n_rows)
            w_out = od_w_out[j]

        xl, h = _resproj_call([mix, att], w_out, xl, mod, 2, mod, (3, 4), None,
                              n_rows=n_rows, tm=512, kc=256, wc=D, name="out_proj")
        hid = _ffn_up_call(h, ffn_w_up[i], ffn_conv_w[i], ffn_conv_b[i], n_rows)
        if need_ctx:
            xl, h = _resproj_call([hid], ffn_w_down[i], xl, mod, 5, mod_all[i + 1], (0, 1), None,
                                  n_rows=n_rows, tm=256, kc=688, wc=D // 2, name="ffn_down")
        else:
            out = _resproj_call([hid], ffn_w_down[i], xl, mod, 5, None, None, final_norm,
                                n_rows=n_rows, tm=256, kc=688, wc=D // 2, name="ffn_down_final")
    return out.reshape(NB, S, D)
```

```python
import functools
import math

import jax
import jax.numpy as jnp
from jax import lax
from jax.experimental import pallas as pl
from jax.experimental.pallas import tpu as pltpu

F32 = jnp.float32
BF16 = jnp.bfloat16

D = 2048
NB = 4
S = 2048
DEPTH = 4
GRID_W = 64
CL = 256
ROPE_THETA = 10000.0
EPS = 1e-6

N_FG = 4
FG_W = 128
FOURIER_W = 512
HEAD_DIM = 128
N_Q_HEADS = 12
N_KV_HEADS = 4
GQA_G = N_Q_HEADS // N_KV_HEADS
Q_W = 1536
KV_W = 512
EVEN_IN_W = 3072

CONV_W = 512
CONV_K = 31
MLA_HEADS = 12
MLA_NOPE = 128
MLA_ROPE = 64
MLA_V = 128
Q_LORA = 512
KV_LORA = 512
MLA_QK = 256

D_FF = 5504
FFN_K = 3

ML = NB * S
MC = NB * CL
MT = ML + MC

LANE = 128
SUBLANES = 8
VMEM_LIMIT = 56 * 1024 * 1024

LOG2E = math.log2(math.e)
QS_GQA = HEAD_DIM ** -0.5 * LOG2E
QS_MLA = (MLA_NOPE + MLA_ROPE) ** -0.5 * LOG2E


def _cparams(n_axes):
    return pltpu.CompilerParams(
        dimension_semantics=("arbitrary",) * n_axes, vmem_limit_bytes=VMEM_LIMIT)


def _seg(m, tm):
    return jnp.minimum((m * tm) // S, NB)


def _mod_spec(layer, which, tm, m_axis):
    def idx(*g):
        return (layer, _seg(g[m_axis], tm), which, 0, 0)
    return pl.BlockSpec((None, None, None, 1, D), idx)


def _rope_idx(m, tm):
    per_seq = S // tm
    n_lat = ML // tm
    return jnp.where(m < n_lat, m % per_seq, per_seq + (m - n_lat))


def _modulate(x, sh, sc):
    ms = jnp.mean(x * x, axis=-1, keepdims=True)
    return x * lax.rsqrt(ms + EPS) * (1.0 + sc) + sh


def _rope128(y, cos, sin):
    return y * cos + pltpu.roll(y, 64, 1) * sin


def _ada_kernel(c_ref, w_ref, b_ref, o_ref):
    c = c_ref[...]
    s = (c * jax.nn.sigmoid(c)).astype(BF16)
    o_ref[...] = jnp.dot(s, w_ref[...].astype(BF16), preferred_element_type=F32) + b_ref[...]


def _ada_call(cvec, ada_w, ada_b):
    tn = 1024
    return pl.pallas_call(
        _ada_kernel,
        grid=(DEPTH, 6 * D // tn),
        in_specs=[
            pl.BlockSpec((8, D), lambda l, j: (0, 0)),
            pl.BlockSpec((None, D, tn), lambda l, j: (l, 0, j)),
            pl.BlockSpec((None, 1, tn), lambda l, j: (l, 0, j)),
        ],
        out_specs=pl.BlockSpec((None, 8, tn), lambda l, j: (l, 0, j)),
        out_shape=jax.ShapeDtypeStruct((DEPTH, 8, 6 * D), F32),
        compiler_params=_cparams(2),
        name="ada_ln",
    )(cvec, ada_w, ada_b.reshape(DEPTH, 1, 6 * D))


def _init_kernel(x_ref, c_ref, sh_ref, sc_ref, xo_ref, h_ref, *, n_lat):
    m = pl.program_id(0)

    @pl.when(m < n_lat)
    def _():
        xo_ref[...] = x_ref[...]

    @pl.when(m >= n_lat)
    def _():
        xo_ref[...] = c_ref[...]

    h_ref[...] = _modulate(xo_ref[...], sh_ref[...], sc_ref[...]).astype(BF16)


def _init_call(x2, c2, mod):
    tm = 512
    n_lat = ML // tm
    return pl.pallas_call(
        functools.partial(_init_kernel, n_lat=n_lat),
        grid=(MT // tm,),
        in_specs=[
            pl.BlockSpec((tm, D), lambda m: (jnp.minimum(m, n_lat - 1), 0)),
            pl.BlockSpec((tm, D), lambda m: (jnp.maximum(m - n_lat, 0), 0)),
            _mod_spec(0, 0, tm, 0),
            _mod_spec(0, 1, tm, 0),
        ],
        out_specs=[pl.BlockSpec((tm, D), lambda m: (m, 0)),
                   pl.BlockSpec((tm, D), lambda m: (m, 0))],
        out_shape=[jax.ShapeDtypeStruct((MT, D), F32), jax.ShapeDtypeStruct((MT, D), BF16)],
        compiler_params=_cparams(1),
        name="init_modulate",
    )(x2, c2, mod, mod)


def _cast_weight(w_ref, wb_ref):
    @pl.when(pl.program_id(1) == 0)
    def _():
        wb_ref[...] = w_ref[...].astype(BF16)


def _even_inproj_kernel(x_ref, w_ref, gain_ref, cos_ref, sin_ref, o_ref, wb_ref, *, tn):
    n = pl.program_id(0)
    _cast_weight(w_ref, wb_ref)
    acc = jnp.dot(x_ref[...], wb_ref[...], preferred_element_type=F32)
    plain = jnp.logical_or(n == 0, n == 5)

    @pl.when(plain)
    def _():
        o_ref[...] = acc.astype(BF16)

    @pl.when(jnp.logical_not(plain))
    def _():
        qs = jnp.where(n < 4, QS_GQA, 1.0)
        for j in range(tn // HEAD_DIM):
            sl = slice(j * HEAD_DIM, (j + 1) * HEAD_DIM)
            a = acc[:, sl]
            g = gain_ref[:, sl] * qs
            y = a * lax.rsqrt(jnp.mean(a * a, axis=-1, keepdims=True) + EPS) * g
            o_ref[:, sl] = _rope128(y, cos_ref[...], sin_ref[...]).astype(BF16)


def _even_inproj_call(h, w_all, layer, gain_full, cos_t, sin_t):
    tm, tn = 1024, 512
    def out_idx(n, m):
        return (m, jnp.where(n == 0, 3, jnp.where(n < 4, n - 1, n)))
    return pl.pallas_call(
        functools.partial(_even_inproj_kernel, tn=tn),
        grid=(EVEN_IN_W // tn, MT // tm),
        in_specs=[
            pl.BlockSpec((tm, D), lambda n, m: (m, 0)),
            pl.BlockSpec((None, D, tn), lambda n, m: (layer, 0, n)),
            pl.BlockSpec((1, tn), lambda n, m: (0, n)),
            pl.BlockSpec((tm, HEAD_DIM), lambda n, m: (_rope_idx(m, tm), 0)),
            pl.BlockSpec((tm, HEAD_DIM), lambda n, m: (_rope_idx(m, tm), 0)),
        ],
        out_specs=pl.BlockSpec((tm, tn), out_idx),
        out_shape=jax.ShapeDtypeStruct((MT, EVEN_IN_W), BF16),
        scratch_shapes=[pltpu.VMEM((D, tn), BF16)],
        compiler_params=_cparams(2),
        name="even_in_proj",
    )(h, w_all, gain_full, cos_t, sin_t)


def _plain_proj_kernel(x_ref, w_ref, o_ref, wb_ref):
    _cast_weight(w_ref, wb_ref)
    o_ref[...] = jnp.dot(x_ref[...], wb_ref[...], preferred_element_type=F32).astype(BF16)


def _odd_inproj_call(h, w_all, layer):
    tm, tn, n_out = 1024, 512, 2048
    return pl.pallas_call(
        _plain_proj_kernel,
        grid=(n_out // tn, MT // tm),
        in_specs=[
            pl.BlockSpec((tm, D), lambda n, m: (m, 0)),
            pl.BlockSpec((None, D, tn), lambda n, m: (layer, 0, n)),
        ],
        out_specs=pl.BlockSpec((tm, tn), lambda n, m: (m, n)),
        out_shape=jax.ShapeDtypeStruct((MT, n_out), BF16),
        scratch_shapes=[pltpu.VMEM((D, tn), BF16)],
        compiler_params=_cparams(2),
        name="odd_in_proj",
    )(h, w_all)


def _small_proj_kernel(x_ref, g_ref, w_ref, cos_ref, sin_ref, o_ref, wb_ref, *, norm, pre_scale, rope, cw):
    @pl.when(pl.program_id(0) == 0)
    def _():
        wb_ref[...] = w_ref[...].astype(BF16)

    x = x_ref[...]
    if norm:
        xf = x.astype(F32)
        g = g_ref[...] * pre_scale
        x = (xf * lax.rsqrt(jnp.mean(xf * xf, axis=-1, keepdims=True) + EPS) * g).astype(BF16)
    n_out = o_ref.shape[1]
    for j in range(n_out // cw):
        acc = jnp.dot(x, wb_ref[:, j * cw:(j + 1) * cw], preferred_element_type=F32)
        for i in range(cw // LANE):
            col = j * cw + i * LANE
            a = acc[:, i * LANE:(i + 1) * LANE]
            roped = rope == "all" or (rope == "odd" and (col // LANE) % 2 == 1)
            if roped:
                a = _rope128(a, cos_ref[...], sin_ref[...])
            o_ref[:, col:col + LANE] = a.astype(BF16)


def _small_proj_call(x, x_col_block, k_in, gain, w_all, layer, cos_t, sin_t, *, n_rows, norm,
                     pre_scale, rope, name):
    tm = 512
    n_out = w_all.shape[2]
    cw = min(512, n_out)
    return pl.pallas_call(
        functools.partial(_small_proj_kernel, norm=norm, pre_scale=pre_scale, rope=rope, cw=cw),
        grid=(n_rows // tm,),
        in_specs=[
            pl.BlockSpec((tm, k_in), lambda m: (m, x_col_block)),
            pl.BlockSpec((1, k_in), lambda m: (0, 0)),
            pl.BlockSpec((None, k_in, n_out), lambda m: (layer, 0, 0)),
            pl.BlockSpec((tm, LANE), lambda m: (_rope_idx(m, tm), 0)),
            pl.BlockSpec((tm, LANE), lambda m: (_rope_idx(m, tm), 0)),
        ],
        out_specs=pl.BlockSpec((tm, n_out), lambda m: (m, 0)),
        out_shape=jax.ShapeDtypeStruct((n_rows, n_out), BF16),
        scratch_shapes=[pltpu.VMEM((k_in, n_out), BF16)],
        compiler_params=_cparams(1),
        name=name,
    )(x, gain, w_all, cos_t, sin_t)


ATTN_KC = CL
ATTN_TQ = 1024


def _attend_heads(heads, n_chunks):
    qk = (((1,), (1,)), ((), ()))
    pv_dn = (((0,), (0,)), ((), ()))

    def fold(a, op):
        return op(a.reshape(a.shape[0] // SUBLANES, SUBLANES, a.shape[1]), axis=0)

    for q, key_chunk, val_chunk, write in heads:
        scores, mx = [], None
        for c in range(n_chunks):
            s = lax.dot_general(key_chunk(c), q, qk, preferred_element_type=F32)
            scores.append(s)
            part = fold(s, jnp.max)
            mx = part if mx is None else jnp.maximum(mx, part)
        m = jnp.max(mx, axis=0, keepdims=True)
        den = acc = None
        for c in range(n_chunks):
            p = jnp.exp2(scores[c] - m)
            part = fold(p, jnp.sum)
            den = part if den is None else den + part
            pv = lax.dot_general(val_chunk(c), p.astype(BF16), pv_dn, preferred_element_type=F32)
            acc = pv if acc is None else acc + pv
        write((acc / jnp.sum(den, axis=0, keepdims=True)).T)


def _chunk(lat_ref, ctx_ref, c, n_lat, cols):
    if c < n_lat:
        return lat_ref[c * ATTN_KC:(c + 1) * ATTN_KC, cols]
    return ctx_ref[:, cols]


def _gqa_kernel(*refs, has_lat):
    if has_lat:
        q_ref, kl_ref, kc_ref, vl_ref, vc_ref, o_ref = refs
    else:
        q_ref, kc_ref, vc_ref, o_ref = refs
        kl_ref = vl_ref = None
    n_lat = S // ATTN_KC if has_lat else 0
    full = slice(None)
    heads = []
    for g in range(GQA_G):
        cols = slice(g * HEAD_DIM, (g + 1) * HEAD_DIM)

        def write(o, cols=cols):
            o_ref[:, cols] = o.astype(BF16)

        heads.append((q_ref[:, cols],
                      lambda c: _chunk(kl_ref, kc_ref, c, n_lat, full),
                      lambda c: _chunk(vl_ref, vc_ref, c, n_lat, full), write))
    _attend_heads(heads, n_lat + 1)


def _gqa_calls(z, need_ctx):
    tq = ATTN_TQ
    k_blk0 = (Q_W + FOURIER_W) // HEAD_DIM
    v_blk0 = k_blk0 + N_KV_HEADS
    ctx_blk0 = ML // CL
    qw = GQA_G * HEAD_DIM
    lat = pl.pallas_call(
        functools.partial(_gqa_kernel, has_lat=True),
        grid=(NB, N_KV_HEADS, S // tq),
        in_specs=[
            pl.BlockSpec((tq, qw), lambda b, h, i: (b * (S // tq) + i, h)),
            pl.BlockSpec((S, HEAD_DIM), lambda b, h, i: (b, k_blk0 + h)),
            pl.BlockSpec((CL, HEAD_DIM), lambda b, h, i: (ctx_blk0 + b, k_blk0 + h)),
            pl.BlockSpec((S, HEAD_DIM), lambda b, h, i: (b, v_blk0 + h)),
            pl.BlockSpec((CL, HEAD_DIM), lambda b, h, i: (ctx_blk0 + b, v_blk0 + h)),
        ],
        out_specs=pl.BlockSpec((tq, qw), lambda b, h, i: (b * (S // tq) + i, h)),
        out_shape=jax.ShapeDtypeStruct((ML, Q_W), BF16),
        compiler_params=_cparams(3),
        name="gqa_lat",
    )(z, z, z, z, z)
    if not need_ctx:
        return lat, None
    ctx = pl.pallas_call(
        functools.partial(_gqa_kernel, has_lat=False),
        grid=(NB, N_KV_HEADS),
        in_specs=[
            pl.BlockSpec((CL, qw), lambda b, h: (ctx_blk0 + b, h)),
            pl.BlockSpec((CL, HEAD_DIM), lambda b, h: (ctx_blk0 + b, k_blk0 + h)),
            pl.BlockSpec((CL, HEAD_DIM), lambda b, h: (ctx_blk0 + b, v_blk0 + h)),
        ],
        out_specs=pl.BlockSpec((CL, qw), lambda b, h: (b, h)),
        out_shape=jax.ShapeDtypeStruct((MC, Q_W), BF16),
        compiler_params=_cparams(2),
        name="gqa_ctx",
    )(z, z, z)
    return lat, ctx


MLA_HPB = 3


def _mla_kernel(*refs, has_lat):
    if has_lat:
        q_ref, kvl_ref, kvc_ref, krl_ref, krc_ref, o_ref = refs
    else:
        q_ref, kvc_ref, krc_ref, o_ref = refs
        kvl_ref = krl_ref = None
    n_lat = S // ATTN_KC if has_lat else 0
    full = slice(None)
    hw = MLA_NOPE + MLA_V
    heads = []
    for g in range(MLA_HPB):
        def key_chunk(c, g=g):
            nope = _chunk(kvl_ref, kvc_ref, c, n_lat, slice(g * hw, g * hw + MLA_NOPE))
            return jnp.concatenate([nope, _chunk(krl_ref, krc_ref, c, n_lat, full)], axis=1)

        def val_chunk(c, g=g):
            return _chunk(kvl_ref, kvc_ref, c, n_lat, slice(g * hw + MLA_NOPE, (g + 1) * hw))

        def write(o, g=g):
            o_ref[:, g * MLA_V:(g + 1) * MLA_V] = o.astype(BF16)

        heads.append((q_ref[:, g * MLA_QK:(g + 1) * MLA_QK], key_chunk, val_chunk, write))
    _attend_heads(heads, n_lat + 1)


def _mla_calls(q, kv, kr, need_ctx):
    tq = ATTN_TQ
    hb = MLA_HEADS // MLA_HPB
    qw, kvw, ow = MLA_HPB * MLA_QK, MLA_HPB * (MLA_NOPE + MLA_V), MLA_HPB * MLA_V
    ctx_blk0 = ML // CL
    lat = pl.pallas_call(
        functools.partial(_mla_kernel, has_lat=True),
        grid=(NB, hb, S // tq),
        in_specs=[
            pl.BlockSpec((tq, qw), lambda b, h, i: (b * (S // tq) + i, h)),
            pl.BlockSpec((S, kvw), lambda b, h, i: (b, h)),
            pl.BlockSpec((CL, kvw), lambda b, h, i: (ctx_blk0 + b, h)),
            pl.BlockSpec((S, LANE), lambda b, h, i: (b, 0)),
            pl.BlockSpec((CL, LANE), lambda b, h, i: (ctx_blk0 + b, 0)),
        ],
        out_specs=pl.BlockSpec((tq, ow), lambda b, h, i: (b * (S // tq) + i, h)),
        out_shape=jax.ShapeDtypeStruct((ML, MLA_HEADS * MLA_V), BF16),
        compiler_params=_cparams(3),
        name="mla_lat",
    )(q, kv, kv, kr, kr)
    if not need_ctx:
        return lat, None
    ctx = pl.pallas_call(
        functools.partial(_mla_kernel, has_lat=False),
        grid=(NB, hb),
        in_specs=[
            pl.BlockSpec((CL, qw), lambda b, h: (ctx_blk0 + b, h)),
            pl.BlockSpec((CL, kvw), lambda b, h: (ctx_blk0 + b, h)),
            pl.BlockSpec((CL, LANE), lambda b, h: (ctx_blk0 + b, 0)),
        ],
        out_specs=pl.BlockSpec((CL, ow), lambda b, h: (b, h)),
        out_shape=jax.ShapeDtypeStruct((MC, MLA_HEADS * MLA_V), BF16),
        compiler_params=_cparams(2),
        name="mla_ctx",
    )(q, kv, kr)
    return lat, ctx


def _fourier_kernel(x_ref, wc_ref, cl_ref, sl_ref, o_ref, ya_ref, yb_ref):
    @pl.when(pl.program_id(1) == 0)
    def _():
        for g in range(N_FG):
            sl = slice(g * FG_W, (g + 1) * FG_W)
            y = jnp.dot(x_ref[:, sl], wc_ref[...], preferred_element_type=F32)
            ya_ref[:, sl] = y[:, :FG_W].astype(BF16)
            yb_ref[:, sl] = y[:, FG_W:].astype(BF16)

    o = (jnp.dot(cl_ref[...], ya_ref[...], preferred_element_type=F32)
         - jnp.dot(sl_ref[...], yb_ref[...], preferred_element_type=F32))
    o_ref[...] = o.astype(BF16)


def _dft_mats(n):
    hi = 64 if n % 64 == 0 and n > 64 else 1
    l = jnp.arange(n, dtype=jnp.int32)[:, None]

    def trig(step, count):
        k = (l * (jnp.arange(count, dtype=jnp.int32)[None, :] * step)) % n
        ang = k.astype(F32) * (2.0 * jnp.pi / n)
        return jnp.cos(ang), jnp.sin(ang)

    nrm = float(n) ** -0.5
    if hi == 1:
        c, s = trig(1, n)
        return c * nrm, s * nrm
    ca, sa = trig(hi, n // hi)
    cb, sb = trig(1, hi)
    ca, sa = ca[:, :, None] * nrm, sa[:, :, None] * nrm
    cb, sb = cb[:, None, :], sb[:, None, :]
    return (ca * cb - sa * sb).reshape(n, n), (sa * cb + ca * sb).reshape(n, n)


def _fourier_calls(z, f_col_block, need_ctx):
    cc, sc = _dft_mats(FG_W)
    wc = jnp.concatenate([cc, sc], axis=1).astype(BF16)
    cl, sl = _dft_mats(S)
    cl, sl = cl.astype(BF16), sl.astype(BF16)
    tr = 512
    lat = pl.pallas_call(
        _fourier_kernel,
        grid=(NB, S // tr),
        in_specs=[
            pl.BlockSpec((S, FOURIER_W), lambda b, r: (b, f_col_block)),
            pl.BlockSpec((FG_W, 2 * FG_W), lambda b, r: (0, 0)),
            pl.BlockSpec((tr, S), lambda b, r: (r, 0)),
            pl.BlockSpec((tr, S), lambda b, r: (r, 0)),
        ],
        out_specs=pl.BlockSpec((tr, FOURIER_W), lambda b, r: (b * (S // tr) + r, 0)),
        out_shape=jax.ShapeDtypeStruct((ML, FOURIER_W), BF16),
        scratch_shapes=[pltpu.VMEM((S, FOURIER_W), BF16), pltpu.VMEM((S, FOURIER_W), BF16)],
        compiler_params=_cparams(2),
        name="fourier_lat",
    )(z, wc, cl, sl)
    if not need_ctx:
        return lat, None
    clc, slc = _dft_mats(CL)
    clc, slc = clc.astype(BF16), slc.astype(BF16)
    ctx_blk0 = ML // CL
    ctx = pl.pallas_call(
        _fourier_kernel,
        grid=(NB, 1),
        in_specs=[
            pl.BlockSpec((CL, FOURIER_W), lambda b, r: (ctx_blk0 + b, f_col_block)),
            pl.BlockSpec((FG_W, 2 * FG_W), lambda b, r: (0, 0)),
            pl.BlockSpec((CL, CL), lambda b, r: (0, 0)),
            pl.BlockSpec((CL, CL), lambda b, r: (0, 0)),
        ],
        out_specs=pl.BlockSpec((CL, FOURIER_W), lambda b, r: (b, 0)),
        out_shape=jax.ShapeDtypeStruct((MC, FOURIER_W), BF16),
        scratch_shapes=[pltpu.VMEM((CL, FOURIER_W), BF16), pltpu.VMEM((CL, FOURIER_W), BF16)],
        compiler_params=_cparams(2),
        name="fourier_ctx",
    )(z, wc, clc, slc)
    return lat, ctx


CONV_TM = 256
CONV_HALO = 16
CONV_CHUNK = 32


def _conv_kernel(a_ref, g_ref, ap_ref, gp_ref, an_ref, gn_ref, w_ref, b_ref, lg_ref, lb_ref,
                 o_ref, ext_ref, *, n_lat, per_seq):
    m = pl.program_id(0)
    is_ctx = m >= n_lat
    first = jnp.logical_or(is_ctx, m % per_seq == 0)
    last = jnp.logical_or(is_ctx, m % per_seq == per_seq - 1)

    def glu(a, g):
        return a.astype(F32) * jax.nn.sigmoid(g.astype(F32))

    tm, hl = CONV_TM, CONV_HALO
    ext_ref[0:hl, :] = jnp.where(first, 0.0, glu(ap_ref[...], gp_ref[...]))
    ext_ref[hl:hl + tm, :] = glu(a_ref[...], g_ref[...])
    ext_ref[hl + tm:, :] = jnp.where(last, 0.0, glu(an_ref[...], gn_ref[...]))

    pad = CONV_K // 2
    for c in range(tm // CONV_CHUNK):
        base = c * CONV_CHUNK + hl - pad
        acc = jnp.zeros((CONV_CHUNK, CONV_W), F32) + b_ref[...]
        for k in range(CONV_K):
            acc = acc + w_ref[k:k + 1, :] * ext_ref[base + k:base + k + CONV_CHUNK, :]
        mu = jnp.mean(acc, axis=-1, keepdims=True)
        d = acc - mu
        var = jnp.mean(d * d, axis=-1, keepdims=True)
        y = d * lax.rsqrt(var + EPS) * lg_ref[...] + lb_ref[...]
        o_ref[c * CONV_CHUNK:(c + 1) * CONV_CHUNK, :] = (y * jax.nn.sigmoid(y)).astype(BF16)


def _conv_call(z, conv_w, conv_b, ln_g, ln_b, n_rows):
    tm, hl = CONV_TM, CONV_HALO
    r = tm // hl
    n_halo_blocks = z.shape[0] // hl
    prev = lambda m: jnp.maximum(m * r - 1, 0)
    nxt = lambda m: jnp.minimum((m + 1) * r, n_halo_blocks - 1)
    row = lambda v: v.reshape(1, CONV_W)
    return pl.pallas_call(
        functools.partial(_conv_kernel, n_lat=ML // tm, per_seq=S // tm),
        grid=(n_rows // tm,),
        in_specs=[
            pl.BlockSpec((tm, CONV_W), lambda m: (m, 0)),
            pl.BlockSpec((tm, CONV_W), lambda m: (m, 1)),
            pl.BlockSpec((hl, CONV_W), lambda m: (prev(m), 0)),
            pl.BlockSpec((hl, CONV_W), lambda m: (prev(m), 1)),
            pl.BlockSpec((hl, CONV_W), lambda m: (nxt(m), 0)),
            pl.BlockSpec((hl, CONV_W), lambda m: (nxt(m), 1)),
            pl.BlockSpec((CONV_K, CONV_W), lambda m: (0, 0)),
            pl.BlockSpec((1, CONV_W), lambda m: (0, 0)),
            pl.BlockSpec((1, CONV_W), lambda m: (0, 0)),
            pl.BlockSpec((1, CONV_W), lambda m: (0, 0)),
        ],
        out_specs=pl.BlockSpec((tm, CONV_W), lambda m: (m, 0)),
        out_shape=jax.ShapeDtypeStruct((n_rows, CONV_W), BF16),
        scratch_shapes=[pltpu.VMEM((tm + 2 * hl, CONV_W), F32)],
        compiler_params=_cparams(1),
        name="conformer_conv",
    )(z, z, z, z, z, z, conv_w, row(conv_b), row(ln_g), row(ln_b))


def _resproj_kernel(*refs, ks, pairs, layer, n_lat, kc, wc, final):
    pos = 0
    lhs = []
    for is_pair in pairs:
        cnt = 2 if is_pair else 1
        lhs.append(refs[pos:pos + cnt])
        pos += cnt
    w_hbm, x_ref, gate_ref = refs[pos:pos + 3]
    pos += 3
    if final:
        gain_ref, o_ref = refs[pos:pos + 2]
        wb_ref, stage_ref, sem = refs[pos + 2:]
    else:
        sh_ref, sc_ref, xo_ref, h_ref = refs[pos:pos + 4]
        wb_ref, stage_ref, sem = refs[pos + 4:]
    k_total = sum(ks)
    m = pl.program_id(0)

    @pl.when(m == 0)
    def _():
        chunks = [(r, c) for r in range(k_total // kc) for c in range(D // wc)]

        def copy(i):
            r, c = chunks[i]
            return pltpu.make_async_copy(
                w_hbm.at[layer, pl.ds(r * kc, kc), pl.ds(c * wc, wc)], stage_ref.at[i % 2], sem.at[i % 2])

        copy(0).start()
        for i, (r, c) in enumerate(chunks):
            if i + 1 < len(chunks):
                copy(i + 1).start()
            copy(i).wait()
            wb_ref[r * kc:(r + 1) * kc, c * wc:(c + 1) * wc] = stage_ref[i % 2].astype(BF16)

    off = 0
    acc = None
    for part, k in zip(lhs, ks):
        a = part[0][...]
        if len(part) == 2:
            a = jnp.where(m < n_lat, a, part[1][...])
        prod = jnp.dot(a, wb_ref[off:off + k, :], preferred_element_type=F32)
        acc = prod if acc is None else acc + prod
        off += k
    xn = x_ref[...] + gate_ref[...] * acc
    if final:
        ms = jnp.mean(xn * xn, axis=-1, keepdims=True)
        o_ref[...] = xn * lax.rsqrt(ms + EPS) * gain_ref[...]
    else:
        xo_ref[...] = xn
        h_ref[...] = _modulate(xn, sh_ref[...], sc_ref[...]).astype(BF16)


def _resproj_call(lhs, w_all, layer, x, mod, mod_layer, gate_idx, next_layer, next_idx, final_gain, *,
                  n_rows, tm, kc, wc, name):
    n_lat = ML // tm
    pairs = tuple(isinstance(a, tuple) for a in lhs)
    ks = tuple((a[0] if p else a).shape[1] for a, p in zip(lhs, pairs))
    k_total = sum(ks)
    final = final_gain is not None
    in_specs, args = [], []
    for a, p, k in zip(lhs, pairs, ks):
        if p:
            in_specs += [pl.BlockSpec((tm, k), lambda m: (jnp.minimum(m, n_lat - 1), 0)),
                         pl.BlockSpec((tm, k), lambda m: (jnp.maximum(m - n_lat, 0), 0))]
            args += [a[0], a[1]]
        else:
            in_specs.append(pl.BlockSpec((tm, k), lambda m: (m, 0)))
            args.append(a)
    in_specs += [pl.BlockSpec(memory_space=pl.ANY),
                 pl.BlockSpec((tm, D), lambda m: (m, 0)),
                 _mod_spec(mod_layer, gate_idx, tm, 0)]
    args += [w_all, x, mod]
    if final:
        in_specs.append(pl.BlockSpec((1, D), lambda m: (0, 0)))
        args.append(final_gain.reshape(1, D))
        out_specs = pl.BlockSpec((tm, D), lambda m: (m, 0))
        out_shape = jax.ShapeDtypeStruct((n_rows, D), F32)
    else:
        in_specs += [_mod_spec(next_layer, next_idx[0], tm, 0), _mod_spec(next_layer, next_idx[1], tm, 0)]
        args += [mod, mod]
        out_specs = [pl.BlockSpec((tm, D), lambda m: (m, 0)), pl.BlockSpec((tm, D), lambda m: (m, 0))]
        out_shape = [jax.ShapeDtypeStruct((n_rows, D), F32), jax.ShapeDtypeStruct((n_rows, D), BF16)]
    return pl.pallas_call(
        functools.partial(_resproj_kernel, ks=ks, pairs=pairs, layer=layer, n_lat=n_lat, kc=kc, wc=wc,
                          final=final),
        grid=(n_rows // tm,),
        in_specs=in_specs,
        out_specs=out_specs,
        out_shape=out_shape,
        scratch_shapes=[pltpu.VMEM((k_total, D), BF16),
                        pltpu.VMEM((2, kc, wc), F32),
                        pltpu.SemaphoreType.DMA((2,))],
        compiler_params=_cparams(1),
        name=name,
    )(*args)


FFN_HALO = 16


def _ffn_up_kernel(x_ref, xp_ref, xn_ref, wg_ref, wu0_ref, wu1_ref, cw_ref, cb_ref, o_ref,
                   wgb_ref, wub_ref, xe_ref, *, tm, tn, u_skip):
    n = pl.program_id(0)
    m = pl.program_id(1)

    @pl.when(m == 0)
    def _():
        wgb_ref[...] = wg_ref[...].astype(BF16)
        wub_ref[:, :tn - u_skip] = wu0_ref[:, u_skip:].astype(BF16)
        wub_ref[:, tn - u_skip:] = wu1_ref[:, :u_skip].astype(BF16)

    hl = FFN_HALO
    xe_ref[0:hl, :] = xp_ref[...]
    xe_ref[hl:hl + tm, :] = x_ref[...]
    xe_ref[hl + tm:, :] = xn_ref[...]
    ge = jnp.dot(xe_ref[...], wgb_ref[...], preferred_element_type=F32)
    u = jnp.dot(x_ref[...], wub_ref[...], preferred_element_type=F32)
    rows = tm + 2 * hl
    g_prev = pltpu.roll(ge, 1, 0)[hl:hl + tm]
    g_next = pltpu.roll(ge, rows - 1, 0)[hl:hl + tm]
    g_mid = ge[hl:hl + tm]
    r = m * tm + lax.broadcasted_iota(jnp.int32, (tm, tn), 0)
    seq = jnp.where(r < ML, S, CL)
    t = jnp.bitwise_and(r, seq - 1)
    g_prev = jnp.where(t == 0, 0.0, g_prev)
    g_next = jnp.where(t == seq - 1, 0.0, g_next)
    conv = cw_ref[0:1, :] * g_prev + cw_ref[1:2, :] * g_mid + cw_ref[2:3, :] * g_next + cb_ref[...]
    val = conv * jax.nn.sigmoid(conv) * u
    col = n * tn + lax.broadcasted_iota(jnp.int32, (tm, tn), 1)
    o_ref[...] = jnp.where(col < D_FF, val, 0.0).astype(BF16)


def _ffn_up_call(h, w_all, layer, conv_w, conv_b, n_rows):
    tm, tn, hl = 1024, 512, FFN_HALO
    r = tm // hl
    n_halo_blocks = n_rows // hl
    u_blk0, u_skip = divmod(D_FF, tn)
    last_blk = pl.cdiv(2 * D_FF, tn) - 1
    return pl.pallas_call(
        functools.partial(_ffn_up_kernel, tm=tm, tn=tn, u_skip=u_skip),
        grid=(pl.cdiv(D_FF, tn), n_rows // tm),
        in_specs=[
            pl.BlockSpec((tm, D), lambda n, m: (m, 0)),
            pl.BlockSpec((hl, D), lambda n, m: (jnp.maximum(m * r - 1, 0), 0)),
            pl.BlockSpec((hl, D), lambda n, m: (jnp.minimum((m + 1) * r, n_halo_blocks - 1), 0)),
            pl.BlockSpec((None, D, tn), lambda n, m: (layer, 0, n)),
            pl.BlockSpec((None, D, tn), lambda n, m: (layer, 0, u_blk0 + n)),
            pl.BlockSpec((None, D, tn), lambda n, m: (layer, 0, jnp.minimum(u_blk0 + n + 1, last_blk))),
            pl.BlockSpec((FFN_K, tn), lambda n, m: (0, n)),
            pl.BlockSpec((1, tn), lambda n, m: (0, n)),
        ],
        out_specs=pl.BlockSpec((tm, tn), lambda n, m: (m, n)),
        out_shape=jax.ShapeDtypeStruct((n_rows, D_FF), BF16),
        scratch_shapes=[pltpu.VMEM((D, tn), BF16), pltpu.VMEM((D, tn), BF16),
                        pltpu.VMEM((tm + 2 * hl, D), BF16)],
        compiler_params=_cparams(2),
        name="ffn_up",
    )(h, h, h, w_all, w_all, w_all, conv_w, conv_b.reshape(1, D_FF))


def _axial_rope_tables(length, rot_dim):
    rows = length // GRID_W
    row = jnp.repeat(jnp.arange(rows, dtype=F32), GRID_W)
    col = jnp.tile(jnp.arange(GRID_W, dtype=F32), rows)
    n = rot_dim // 4
    inv = jnp.power(ROPE_THETA, -jnp.arange(n, dtype=F32) / n)
    ang = jnp.concatenate([row[:, None] * inv, col[:, None] * inv], axis=-1)
    return jnp.cos(ang), jnp.sin(ang)


def _rope_tables_gqa():
    c, s = _axial_rope_tables(S, HEAD_DIM)
    cos = jnp.concatenate([c, c], axis=1)
    sin = jnp.concatenate([-s, s], axis=1)
    ident_c = jnp.ones((MC, LANE), F32)
    ident_s = jnp.zeros((MC, LANE), F32)
    return jnp.concatenate([cos, ident_c], axis=0), jnp.concatenate([sin, ident_s], axis=0)


def _rope_tables_mla():
    c, s = _axial_rope_tables(S, MLA_ROPE)
    z = jnp.zeros_like(c)
    cos = jnp.concatenate([c, z, c, z], axis=1)
    sin = jnp.concatenate([-s, z, s, z], axis=1)
    one = jnp.ones((MC, 32), F32)
    zc = jnp.zeros((MC, 32), F32)
    ident_c = jnp.concatenate([one, zc, one, zc], axis=1)
    return jnp.concatenate([cos, ident_c], axis=0), jnp.concatenate([sin, jnp.zeros((MC, LANE), F32)], axis=0)


def _spread_rope_cols(w):
    h = MLA_ROPE // 2
    z = jnp.zeros(w.shape[:-1] + (h,), w.dtype)
    return jnp.concatenate([w[..., :h], z, w[..., h:], z], axis=-1)


def kernel(x, c, ctx, c_ctx, ada_w, ada_b, ev_w_in, ev_q_gain, ev_k_gain, ev_w_out, od_w_in, od_conv_w, od_conv_b, od_ln_g, od_ln_b, od_q_norm, od_w_uq, od_kv_norm, od_w_ukv, od_w_out, ffn_w_up, ffn_conv_w, ffn_conv_b, ffn_w_down, final_norm):
    cvec = jnp.concatenate([c, c_ctx[None, :], jnp.zeros((8 - NB - 1, D), F32)], axis=0)
    mod = _ada_call(cvec, ada_w, ada_b).reshape(DEPTH, 8, 6, 1, D)
    cos_a, sin_a = _rope_tables_gqa()
    cos_m, sin_m = _rope_tables_mla()

    n_odd = od_w_in.shape[0]
    w_kr = _spread_rope_cols(od_w_in[:, :, 2 * CONV_W + Q_LORA + KV_LORA:])
    wq = od_w_uq.reshape(n_odd, Q_LORA, MLA_HEADS, MLA_NOPE + MLA_ROPE)
    wq = jnp.concatenate([wq[..., :MLA_NOPE], _spread_rope_cols(wq[..., MLA_NOPE:])], axis=-1)
    wq = wq.reshape(n_odd, Q_LORA, MLA_HEADS * MLA_QK)

    xl, h = _init_call(x.reshape(ML, D), ctx.reshape(MC, D), mod)

    for i in range(DEPTH):
        need_ctx = i < DEPTH - 1
        n_rows = MT if need_ctx else ML
        j = i // 2
        if i % 2 == 0:
            gain_full = jnp.concatenate([
                jnp.ones((FOURIER_W,), F32), jnp.tile(ev_q_gain[j], N_Q_HEADS),
                jnp.tile(ev_k_gain[j], N_KV_HEADS), jnp.ones((KV_W,), F32)]).reshape(1, EVEN_IN_W)
            z = _even_inproj_call(h, ev_w_in, j, gain_full, cos_a, sin_a)
            att = _gqa_calls(z, need_ctx)
            mix = _fourier_calls(z, Q_W // FOURIER_W, need_ctx)
            w_out = ev_w_out
        else:
            z = _odd_inproj_call(h, od_w_in, j)
            ones = jnp.ones((1, D), F32)
            kr = _small_proj_call(h, 0, D, ones, w_kr, j, cos_m, sin_m, n_rows=MT, norm=False,
                                  pre_scale=1.0, rope="all", name="mla_k_rope")
            q = _small_proj_call(z, 2, Q_LORA, od_q_norm[j].reshape(1, Q_LORA), wq, j, cos_m, sin_m,
                                 n_rows=n_rows, norm=True, pre_scale=QS_MLA, rope="odd", name="mla_q_up")
            kv = _small_proj_call(z, 3, KV_LORA, od_kv_norm[j].reshape(1, KV_LORA), od_w_ukv, j,
                                  cos_m, sin_m, n_rows=MT, norm=True, pre_scale=1.0, rope="none",
                                  name="mla_kv_up")
            att = _mla_calls(q, kv, kr, need_ctx)
            mix = _conv_call(z, od_conv_w[j], od_conv_b[j], od_ln_g[j], od_ln_b[j], n_rows)
            w_out = od_w_out

        def rows(v):
            if not isinstance(v, tuple):
                return v
            return v if v[1] is not None else v[0]

        xl, h = _resproj_call([rows(mix), rows(att)], w_out, j, xl, mod, i, 2, i, (3, 4), None,
                              n_rows=n_rows, tm=512, kc=256, wc=D, name="out_proj")
        hid = _ffn_up_call(h, ffn_w_up, i, ffn_conv_w[i], ffn_conv_b[i], n_rows)
        if need_ctx:
            xl, h = _resproj_call([hid], ffn_w_down, i, xl, mod, i, 5, i + 1, (0, 1), None,
                                  n_rows=n_rows, tm=256, kc=688, wc=D // 2, name="ffn_down")
        else:
            out = _resproj_call([hid], ffn_w_down, i, xl, mod, i, 5, None, None, final_norm,
                                n_rows=n_rows, tm=256, kc=688, wc=D // 2, name="ffn_down_final")
    return out.reshape(NB, S, D)
```

```python
import functools
import math

import jax
import jax.numpy as jnp
from jax import lax
from jax.experimental import pallas as pl
from jax.experimental.pallas import tpu as pltpu

F32 = jnp.float32
BF16 = jnp.bfloat16

D = 2048
NB = 4
S = 2048
DEPTH = 4
GRID_W = 64
CL = 256
ROPE_THETA = 10000.0
EPS = 1e-6

N_FG = 4
FG_W = 128
FOURIER_W = 512
HEAD_DIM = 128
N_Q_HEADS = 12
N_KV_HEADS = 4
GQA_G = N_Q_HEADS // N_KV_HEADS
Q_W = 1536
KV_W = 512
EVEN_IN_W = 3072

CONV_W = 512
CONV_K = 31
MLA_HEADS = 12
MLA_NOPE = 128
MLA_ROPE = 64
MLA_V = 128
Q_LORA = 512
KV_LORA = 512
MLA_QK = 256

D_FF = 5504
FFN_K = 3

ML = NB * S
MC = NB * CL
MT = ML + MC

LANE = 128
SUBLANES = 8
VMEM_LIMIT = 56 * 1024 * 1024

LOG2E = math.log2(math.e)
QS_GQA = HEAD_DIM ** -0.5 * LOG2E
QS_MLA = (MLA_NOPE + MLA_ROPE) ** -0.5 * LOG2E


def _cparams(n_axes):
    return pltpu.CompilerParams(
        dimension_semantics=("arbitrary",) * n_axes, vmem_limit_bytes=VMEM_LIMIT)


def _seg(m, tm):
    return jnp.minimum((m * tm) // S, NB)


def _mod_spec(layer, which, tm, m_axis):
    def idx(*g):
        return (layer, _seg(g[m_axis], tm), which, 0, 0)
    return pl.BlockSpec((None, None, None, 1, D), idx)


def _rope_idx(m, tm):
    per_seq = S // tm
    n_lat = ML // tm
    return jnp.where(m < n_lat, m % per_seq, per_seq + (m - n_lat))


def _modulate(x, sh, sc):
    ms = jnp.mean(x * x, axis=-1, keepdims=True)
    return x * lax.rsqrt(ms + EPS) * (1.0 + sc) + sh


def _rope128(y, cos, sin):
    return y * cos + pltpu.roll(y, 64, 1) * sin


def _ada_kernel(c_ref, w_ref, b_ref, o_ref):
    c = c_ref[...]
    s = (c * jax.nn.sigmoid(c)).astype(BF16)
    o_ref[...] = jnp.dot(s, w_ref[...].astype(BF16), preferred_element_type=F32) + b_ref[...]


def _ada_call(cvec, ada_w, ada_b):
    tn = 1024
    return pl.pallas_call(
        _ada_kernel,
        grid=(DEPTH, 6 * D // tn),
        in_specs=[
            pl.BlockSpec((8, D), lambda l, j: (0, 0)),
            pl.BlockSpec((None, D, tn), lambda l, j: (l, 0, j)),
            pl.BlockSpec((None, 1, tn), lambda l, j: (l, 0, j)),
        ],
        out_specs=pl.BlockSpec((None, 8, tn), lambda l, j: (l, 0, j)),
        out_shape=jax.ShapeDtypeStruct((DEPTH, 8, 6 * D), F32),
        compiler_params=_cparams(2),
        name="ada_ln",
    )(cvec, ada_w, ada_b.reshape(DEPTH, 1, 6 * D))


def _init_kernel(x_ref, c_ref, sh_ref, sc_ref, xo_ref, h_ref, *, n_lat):
    m = pl.program_id(0)

    @pl.when(m < n_lat)
    def _():
        xo_ref[...] = x_ref[...]

    @pl.when(m >= n_lat)
    def _():
        xo_ref[...] = c_ref[...]

    h_ref[...] = _modulate(xo_ref[...], sh_ref[...], sc_ref[...]).astype(BF16)


def _init_call(x2, c2, mod):
    tm = 512
    n_lat = ML // tm
    return pl.pallas_call(
        functools.partial(_init_kernel, n_lat=n_lat),
        grid=(MT // tm,),
        in_specs=[
            pl.BlockSpec((tm, D), lambda m: (jnp.minimum(m, n_lat - 1), 0)),
            pl.BlockSpec((tm, D), lambda m: (jnp.maximum(m - n_lat, 0), 0)),
            _mod_spec(0, 0, tm, 0),
            _mod_spec(0, 1, tm, 0),
        ],
        out_specs=[pl.BlockSpec((tm, D), lambda m: (m, 0)),
                   pl.BlockSpec((tm, D), lambda m: (m, 0))],
        out_shape=[jax.ShapeDtypeStruct((MT, D), F32), jax.ShapeDtypeStruct((MT, D), BF16)],
        compiler_params=_cparams(1),
        name="init_modulate",
    )(x2, c2, mod, mod)


def _cast_weight(w_ref, wb_ref):
    @pl.when(pl.program_id(1) == 0)
    def _():
        wb_ref[...] = w_ref[...].astype(BF16)


def _even_inproj_kernel(x_ref, w_ref, gain_ref, cos_ref, sin_ref, o_ref, wb_ref, *, tn):
    n = pl.program_id(0)
    _cast_weight(w_ref, wb_ref)
    acc = jnp.dot(x_ref[...], wb_ref[...], preferred_element_type=F32)
    plain = jnp.logical_or(n == 0, n == 5)

    @pl.when(plain)
    def _():
        o_ref[...] = acc.astype(BF16)

    @pl.when(jnp.logical_not(plain))
    def _():
        qs = jnp.where(n < 4, QS_GQA, 1.0)
        for j in range(tn // HEAD_DIM):
            sl = slice(j * HEAD_DIM, (j + 1) * HEAD_DIM)
            a = acc[:, sl]
            g = gain_ref[:, sl] * qs
            y = a * lax.rsqrt(jnp.mean(a * a, axis=-1, keepdims=True) + EPS) * g
            o_ref[:, sl] = _rope128(y, cos_ref[...], sin_ref[...]).astype(BF16)


def _even_inproj_call(h, w_all, layer, gain_full, cos_t, sin_t):
    tm, tn = 1024, 512
    def out_idx(n, m):
        return (m, jnp.where(n == 0, 3, jnp.where(n < 4, n - 1, n)))
    return pl.pallas_call(
        functools.partial(_even_inproj_kernel, tn=tn),
        grid=(EVEN_IN_W // tn, MT // tm),
        in_specs=[
            pl.BlockSpec((tm, D), lambda n, m: (m, 0)),
            pl.BlockSpec((None, D, tn), lambda n, m: (layer, 0, n)),
            pl.BlockSpec((1, tn), lambda n, m: (0, n)),
            pl.BlockSpec((tm, HEAD_DIM), lambda n, m: (_rope_idx(m, tm), 0)),
            pl.BlockSpec((tm, HEAD_DIM), lambda n, m: (_rope_idx(m, tm), 0)),
        ],
        out_specs=pl.BlockSpec((tm, tn), out_idx),
        out_shape=jax.ShapeDtypeStruct((MT, EVEN_IN_W), BF16),
        scratch_shapes=[pltpu.VMEM((D, tn), BF16)],
        compiler_params=_cparams(2),
        name="even_in_proj",
    )(h, w_all, gain_full, cos_t, sin_t)


def _odd_inproj_kernel(x_ref, wt_ref, cos_ref, sin_ref, o_ref, wb_ref, *, n_plain):
    n = pl.program_id(0)
    _cast_weight(wt_ref, wb_ref)
    dn = (((1,), (1,)), ((), ()))

    @pl.when(n < n_plain)
    def _():
        o_ref[...] = lax.dot_general(x_ref[...], wb_ref[...], dn, preferred_element_type=F32).astype(BF16)

    @pl.when(n == n_plain)
    def _():
        a = lax.dot_general(x_ref[...], wb_ref[0:LANE, :], dn, preferred_element_type=F32)
        lane = lax.broadcasted_iota(jnp.int32, a.shape, 1)
        h = MLA_ROPE // 2
        upper = jnp.where(jnp.logical_and(lane >= 2 * h, lane < 3 * h), pltpu.roll(a, h, 1), 0.0)
        a = jnp.where(lane < h, a, upper)
        o_ref[:, 0:LANE] = _rope128(a, cos_ref[...], sin_ref[...]).astype(BF16)
        o_ref[:, LANE:] = jnp.zeros((o_ref.shape[0], o_ref.shape[1] - LANE), BF16)


ODD_Z_W = 2560


def _odd_inproj_call(h, wt_all, layer, cos_t, sin_t):
    tm, tn = 1024, 512
    n_plain = (2 * CONV_W + Q_LORA + KV_LORA) // tn
    return pl.pallas_call(
        functools.partial(_odd_inproj_kernel, n_plain=n_plain),
        grid=(n_plain + 1, MT // tm),
        in_specs=[
            pl.BlockSpec((tm, D), lambda n, m: (m, 0)),
            pl.BlockSpec((None, tn, D), lambda n, m: (layer, n, 0)),
            pl.BlockSpec((tm, LANE), lambda n, m: (_rope_idx(m, tm), 0)),
            pl.BlockSpec((tm, LANE), lambda n, m: (_rope_idx(m, tm), 0)),
        ],
        out_specs=pl.BlockSpec((tm, tn), lambda n, m: (m, n)),
        out_shape=jax.ShapeDtypeStruct((MT, ODD_Z_W), BF16),
        scratch_shapes=[pltpu.VMEM((tn, D), BF16)],
        compiler_params=_cparams(2),
        name="odd_in_proj",
    )(h, wt_all, cos_t, sin_t)


def _small_proj_kernel(x_ref, g_ref, w_ref, cos_ref, sin_ref, o_ref, wb_ref, *, norm, pre_scale, rope, cw):
    @pl.when(pl.program_id(0) == 0)
    def _():
        wb_ref[...] = w_ref[...].astype(BF16)

    x = x_ref[...]
    if norm:
        xf = x.astype(F32)
        g = g_ref[...] * pre_scale
        x = (xf * lax.rsqrt(jnp.mean(xf * xf, axis=-1, keepdims=True) + EPS) * g).astype(BF16)
    n_out = o_ref.shape[1]
    for j in range(n_out // cw):
        acc = jnp.dot(x, wb_ref[:, j * cw:(j + 1) * cw], preferred_element_type=F32)
        for i in range(cw // LANE):
            col = j * cw + i * LANE
            a = acc[:, i * LANE:(i + 1) * LANE]
            roped = rope == "all" or (rope == "odd" and (col // LANE) % 2 == 1)
            if roped:
                a = _rope128(a, cos_ref[...], sin_ref[...])
            o_ref[:, col:col + LANE] = a.astype(BF16)


def _small_proj_call(x, x_col_block, k_in, gain, w_all, layer, cos_t, sin_t, *, n_rows, norm,
                     pre_scale, rope, name):
    tm = 512
    n_out = w_all.shape[2]
    cw = min(512, n_out)
    return pl.pallas_call(
        functools.partial(_small_proj_kernel, norm=norm, pre_scale=pre_scale, rope=rope, cw=cw),
        grid=(n_rows // tm,),
        in_specs=[
            pl.BlockSpec((tm, k_in), lambda m: (m, x_col_block)),
            pl.BlockSpec((1, k_in), lambda m: (0, 0)),
            pl.BlockSpec((None, k_in, n_out), lambda m: (layer, 0, 0)),
            pl.BlockSpec((tm, LANE), lambda m: (_rope_idx(m, tm), 0)),
            pl.BlockSpec((tm, LANE), lambda m: (_rope_idx(m, tm), 0)),
        ],
        out_specs=pl.BlockSpec((tm, n_out), lambda m: (m, 0)),
        out_shape=jax.ShapeDtypeStruct((n_rows, n_out), BF16),
        scratch_shapes=[pltpu.VMEM((k_in, n_out), BF16)],
        compiler_params=_cparams(1),
        name=name,
    )(x, gain, w_all, cos_t, sin_t)


ATTN_KC = CL
ATTN_TQ = 1024


def _attend_heads(heads, n_chunks):
    qk = (((1,), (1,)), ((), ()))
    pv_dn = (((0,), (0,)), ((), ()))

    def fold(a, op):
        return op(a.reshape(a.shape[0] // SUBLANES, SUBLANES, a.shape[1]), axis=0)

    for q, key_chunk, val_chunk, write in heads:
        scores, mx = [], None
        for c in range(n_chunks):
            s = lax.dot_general(key_chunk(c), q, qk, preferred_element_type=F32)
            scores.append(s)
            part = fold(s, jnp.max)
            mx = part if mx is None else jnp.maximum(mx, part)
        m = jnp.max(mx, axis=0, keepdims=True)
        den = acc = None
        for c in range(n_chunks):
            p = jnp.exp2(scores[c] - m)
            part = fold(p, jnp.sum)
            den = part if den is None else den + part
            pv = lax.dot_general(val_chunk(c), p.astype(BF16), pv_dn, preferred_element_type=F32)
            acc = pv if acc is None else acc + pv
        write((acc / jnp.sum(den, axis=0, keepdims=True)).T)


def _chunk(lat_ref, ctx_ref, c, n_lat, cols):
    if c < n_lat:
        return lat_ref[c * ATTN_KC:(c + 1) * ATTN_KC, cols]
    return ctx_ref[:, cols]


def _gqa_kernel(*refs, has_lat, kvh):
    if has_lat:
        q_ref, kl_ref, kc_ref, vl_ref, vc_ref, o_ref = refs
    else:
        q_ref, kc_ref, vc_ref, o_ref = refs
        kl_ref = vl_ref = None
    n_lat = S // ATTN_KC if has_lat else 0
    heads = []
    for hk in range(kvh):
        kcols = slice(hk * HEAD_DIM, (hk + 1) * HEAD_DIM)
        for g in range(GQA_G):
            cols = slice((hk * GQA_G + g) * HEAD_DIM, (hk * GQA_G + g + 1) * HEAD_DIM)

            def write(o, cols=cols):
                o_ref[:, cols] = o.astype(BF16)

            heads.append((q_ref[:, cols],
                          lambda c, kcols=kcols: _chunk(kl_ref, kc_ref, c, n_lat, kcols),
                          lambda c, kcols=kcols: _chunk(vl_ref, vc_ref, c, n_lat, kcols), write))
    _attend_heads(heads, n_lat + 1)


def _gqa_calls(z, need_ctx):
    tq = ATTN_TQ
    k_blk0 = (Q_W + FOURIER_W) // HEAD_DIM
    v_blk0 = k_blk0 + N_KV_HEADS
    ctx_blk0 = ML // CL
    qw = GQA_G * HEAD_DIM
    lat = pl.pallas_call(
        functools.partial(_gqa_kernel, has_lat=True, kvh=1),
        grid=(NB, N_KV_HEADS, S // tq),
        in_specs=[
            pl.BlockSpec((tq, qw), lambda b, h, i: (b * (S // tq) + i, h)),
            pl.BlockSpec((S, HEAD_DIM), lambda b, h, i: (b, k_blk0 + h)),
            pl.BlockSpec((CL, HEAD_DIM), lambda b, h, i: (ctx_blk0 + b, k_blk0 + h)),
            pl.BlockSpec((S, HEAD_DIM), lambda b, h, i: (b, v_blk0 + h)),
            pl.BlockSpec((CL, HEAD_DIM), lambda b, h, i: (ctx_blk0 + b, v_blk0 + h)),
        ],
        out_specs=pl.BlockSpec((tq, qw), lambda b, h, i: (b * (S // tq) + i, h)),
        out_shape=jax.ShapeDtypeStruct((ML, Q_W), BF16),
        compiler_params=_cparams(3),
        name="gqa_lat",
    )(z, z, z, z, z)
    if not need_ctx:
        return lat, None
    ctx = pl.pallas_call(
        functools.partial(_gqa_kernel, has_lat=False, kvh=N_KV_HEADS),
        grid=(NB,),
        in_specs=[
            pl.BlockSpec((CL, Q_W), lambda b: (ctx_blk0 + b, 0)),
            pl.BlockSpec((CL, KV_W), lambda b: (ctx_blk0 + b, k_blk0 * HEAD_DIM // KV_W)),
            pl.BlockSpec((CL, KV_W), lambda b: (ctx_blk0 + b, v_blk0 * HEAD_DIM // KV_W)),
        ],
        out_specs=pl.BlockSpec((CL, Q_W), lambda b: (b, 0)),
        out_shape=jax.ShapeDtypeStruct((MC, Q_W), BF16),
        compiler_params=_cparams(1),
        name="gqa_ctx",
    )(z, z, z)
    return lat, ctx


MLA_HPB = 6


def _mla_kernel(*refs, has_lat, hpb):
    if has_lat:
        q_ref, kvl_ref, kvc_ref, krl_ref, krc_ref, o_ref = refs
    else:
        q_ref, kvc_ref, krc_ref, o_ref = refs
        kvl_ref = krl_ref = None
    n_lat = S // ATTN_KC if has_lat else 0
    full = slice(None)
    hw = MLA_NOPE + MLA_V
    heads = []
    for g in range(hpb):
        def key_chunk(c, g=g):
            nope = _chunk(kvl_ref, kvc_ref, c, n_lat, slice(g * hw, g * hw + MLA_NOPE))
            return jnp.concatenate([nope, _chunk(krl_ref, krc_ref, c, n_lat, full)], axis=1)

        def val_chunk(c, g=g):
            return _chunk(kvl_ref, kvc_ref, c, n_lat, slice(g * hw + MLA_NOPE, (g + 1) * hw))

        def write(o, g=g):
            o_ref[:, g * MLA_V:(g + 1) * MLA_V] = o.astype(BF16)

        heads.append((q_ref[:, g * MLA_QK:(g + 1) * MLA_QK], key_chunk, val_chunk, write))
    _attend_heads(heads, n_lat + 1)


def _mla_calls(q, kv, kr, kr_blk, need_ctx):
    tq = ATTN_TQ
    hb = MLA_HEADS // MLA_HPB
    qw, kvw, ow = MLA_HPB * MLA_QK, MLA_HPB * (MLA_NOPE + MLA_V), MLA_HPB * MLA_V
    ctx_blk0 = ML // CL
    lat = pl.pallas_call(
        functools.partial(_mla_kernel, has_lat=True, hpb=MLA_HPB),
        grid=(NB, hb, S // tq),
        in_specs=[
            pl.BlockSpec((tq, qw), lambda b, h, i: (b * (S // tq) + i, h)),
            pl.BlockSpec((S, kvw), lambda b, h, i: (b, h)),
            pl.BlockSpec((CL, kvw), lambda b, h, i: (ctx_blk0 + b, h)),
            pl.BlockSpec((S, LANE), lambda b, h, i: (b, kr_blk)),
            pl.BlockSpec((CL, LANE), lambda b, h, i: (ctx_blk0 + b, kr_blk)),
        ],
        out_specs=pl.BlockSpec((tq, ow), lambda b, h, i: (b * (S // tq) + i, h)),
        out_shape=jax.ShapeDtypeStruct((ML, MLA_HEADS * MLA_V), BF16),
        compiler_params=_cparams(3),
        name="mla_lat",
    )(q, kv, kv, kr, kr)
    if not need_ctx:
        return lat, None
    ctx = pl.pallas_call(
        functools.partial(_mla_kernel, has_lat=False, hpb=MLA_HEADS),
        grid=(NB,),
        in_specs=[
            pl.BlockSpec((CL, MLA_HEADS * MLA_QK), lambda b: (ctx_blk0 + b, 0)),
            pl.BlockSpec((CL, MLA_HEADS * (MLA_NOPE + MLA_V)), lambda b: (ctx_blk0 + b, 0)),
            pl.BlockSpec((CL, LANE), lambda b: (ctx_blk0 + b, kr_blk)),
        ],
        out_specs=pl.BlockSpec((CL, MLA_HEADS * MLA_V), lambda b: (b, 0)),
        out_shape=jax.ShapeDtypeStruct((MC, MLA_HEADS * MLA_V), BF16),
        compiler_params=_cparams(1),
        name="mla_ctx",
    )(q, kv, kr)
    return lat, ctx


def _fourier_kernel(x_ref, wc_ref, cl_ref, sl_ref, o_ref, ya_ref, yb_ref):
    @pl.when(pl.program_id(1) == 0)
    def _():
        for g in range(N_FG):
            sl = slice(g * FG_W, (g + 1) * FG_W)
            y = jnp.dot(x_ref[:, sl], wc_ref[...], preferred_element_type=F32)
            ya_ref[:, sl] = y[:, :FG_W].astype(BF16)
            yb_ref[:, sl] = y[:, FG_W:].astype(BF16)

    o = (jnp.dot(cl_ref[...], ya_ref[...], preferred_element_type=F32)
         - jnp.dot(sl_ref[...], yb_ref[...], preferred_element_type=F32))
    o_ref[...] = o.astype(BF16)


def _dft_mats(n):
    hi = 64 if n % 64 == 0 and n > 64 else 1
    l = jnp.arange(n, dtype=jnp.int32)[:, None]

    def trig(step, count):
        k = (l * (jnp.arange(count, dtype=jnp.int32)[None, :] * step)) % n
        ang = k.astype(F32) * (2.0 * jnp.pi / n)
        return jnp.cos(ang), jnp.sin(ang)

    nrm = float(n) ** -0.5
    if hi == 1:
        c, s = trig(1, n)
        return c * nrm, s * nrm
    ca, sa = trig(hi, n // hi)
    cb, sb = trig(1, hi)
    ca, sa = ca[:, :, None] * nrm, sa[:, :, None] * nrm
    cb, sb = cb[:, None, :], sb[:, None, :]
    return (ca * cb - sa * sb).reshape(n, n), (sa * cb + ca * sb).reshape(n, n)


def _fourier_calls(z, f_col_block, need_ctx):
    cc, sc = _dft_mats(FG_W)
    wc = jnp.concatenate([cc, sc], axis=1).astype(BF16)
    cl, sl = _dft_mats(S)
    cl, sl = cl.astype(BF16).T, sl.astype(BF16).T
    tr = 512
    lat = pl.pallas_call(
        _fourier_kernel,
        grid=(NB, S // tr),
        in_specs=[
            pl.BlockSpec((S, FOURIER_W), lambda b, r: (b, f_col_block)),
            pl.BlockSpec((FG_W, 2 * FG_W), lambda b, r: (0, 0)),
            pl.BlockSpec((tr, S), lambda b, r: (r, 0)),
            pl.BlockSpec((tr, S), lambda b, r: (r, 0)),
        ],
        out_specs=pl.BlockSpec((tr, FOURIER_W), lambda b, r: (b * (S // tr) + r, 0)),
        out_shape=jax.ShapeDtypeStruct((ML, FOURIER_W), BF16),
        scratch_shapes=[pltpu.VMEM((S, FOURIER_W), BF16), pltpu.VMEM((S, FOURIER_W), BF16)],
        compiler_params=_cparams(2),
        name="fourier_lat",
    )(z, wc, cl, sl)
    if not need_ctx:
        return lat, None
    clc, slc = _dft_mats(CL)
    clc, slc = clc.astype(BF16), slc.astype(BF16)
    ctx_blk0 = ML // CL
    ctx = pl.pallas_call(
        _fourier_kernel,
        grid=(NB, 1),
        in_specs=[
            pl.BlockSpec((CL, FOURIER_W), lambda b, r: (ctx_blk0 + b, f_col_block)),
            pl.BlockSpec((FG_W, 2 * FG_W), lambda b, r: (0, 0)),
            pl.BlockSpec((CL, CL), lambda b, r: (0, 0)),
            pl.BlockSpec((CL, CL), lambda b, r: (0, 0)),
        ],
        out_specs=pl.BlockSpec((CL, FOURIER_W), lambda b, r: (b, 0)),
        out_shape=jax.ShapeDtypeStruct((MC, FOURIER_W), BF16),
        scratch_shapes=[pltpu.VMEM((CL, FOURIER_W), BF16), pltpu.VMEM((CL, FOURIER_W), BF16)],
        compiler_params=_cparams(2),
        name="fourier_ctx",
    )(z, wc, clc, slc)
    return lat, ctx


CONV_TM = 256
CONV_HALO = 16
CONV_CHUNK = 32


def _conv_kernel(a_ref, g_ref, ap_ref, gp_ref, an_ref, gn_ref, w_ref, b_ref, lg_ref, lb_ref,
                 o_ref, ext_ref, *, n_lat, per_seq):
    m = pl.program_id(0)
    is_ctx = m >= n_lat
    first = jnp.logical_or(is_ctx, m % per_seq == 0)
    last = jnp.logical_or(is_ctx, m % per_seq == per_seq - 1)

    def glu(a, g):
        return a.astype(F32) * jax.nn.sigmoid(g.astype(F32))

    tm, hl, sl = CONV_TM, CONV_HALO, SUBLANES
    ext_ref[0, 0:hl, :] = jnp.where(first, 0.0, glu(ap_ref[...], gp_ref[...]))
    ext_ref[0, hl:hl + tm, :] = glu(a_ref[...], g_ref[...])
    ext_ref[0, hl + tm:, :] = jnp.where(last, 0.0, glu(an_ref[...], gn_ref[...]))
    span = tm + 2 * hl - sl
    for s in range(1, sl):
        ext_ref[s, 0:span, :] = ext_ref[0, s:s + span, :]

    pad = CONV_K // 2
    for c in range(tm // CONV_CHUNK):
        base = c * CONV_CHUNK + hl - pad
        acc = jnp.zeros((CONV_CHUNK, CONV_W), F32) + b_ref[...]
        for k in range(CONV_K):
            s, a = (base + k) % sl, (base + k) // sl * sl
            acc = acc + w_ref[k:k + 1, :] * ext_ref[s, a:a + CONV_CHUNK, :]
        mu = jnp.mean(acc, axis=-1, keepdims=True)
        d = acc - mu
        var = jnp.mean(d * d, axis=-1, keepdims=True)
        y = d * lax.rsqrt(var + EPS) * lg_ref[...] + lb_ref[...]
        o_ref[c * CONV_CHUNK:(c + 1) * CONV_CHUNK, :] = (y * jax.nn.sigmoid(y)).astype(BF16)


def _conv_call(z, conv_w, conv_b, ln_g, ln_b, n_rows):
    tm, hl = CONV_TM, CONV_HALO
    r = tm // hl
    n_halo_blocks = z.shape[0] // hl
    prev = lambda m: jnp.maximum(m * r - 1, 0)
    nxt = lambda m: jnp.minimum((m + 1) * r, n_halo_blocks - 1)
    row = lambda v: v.reshape(1, CONV_W)
    return pl.pallas_call(
        functools.partial(_conv_kernel, n_lat=ML // tm, per_seq=S // tm),
        grid=(n_rows // tm,),
        in_specs=[
            pl.BlockSpec((tm, CONV_W), lambda m: (m, 0)),
            pl.BlockSpec((tm, CONV_W), lambda m: (m, 1)),
            pl.BlockSpec((hl, CONV_W), lambda m: (prev(m), 0)),
            pl.BlockSpec((hl, CONV_W), lambda m: (prev(m), 1)),
            pl.BlockSpec((hl, CONV_W), lambda m: (nxt(m), 0)),
            pl.BlockSpec((hl, CONV_W), lambda m: (nxt(m), 1)),
            pl.BlockSpec((CONV_K, CONV_W), lambda m: (0, 0)),
            pl.BlockSpec((1, CONV_W), lambda m: (0, 0)),
            pl.BlockSpec((1, CONV_W), lambda m: (0, 0)),
            pl.BlockSpec((1, CONV_W), lambda m: (0, 0)),
        ],
        out_specs=pl.BlockSpec((tm, CONV_W), lambda m: (m, 0)),
        out_shape=jax.ShapeDtypeStruct((n_rows, CONV_W), BF16),
        scratch_shapes=[pltpu.VMEM((SUBLANES, tm + 2 * hl, CONV_W), F32)],
        compiler_params=_cparams(1),
        name="conformer_conv",
    )(z, z, z, z, z, z, conv_w, row(conv_b), row(ln_g), row(ln_b))


def _resproj_kernel(*refs, ks, pairs, layer, n_lat, kc, wc, final):
    pos = 0
    lhs = []
    for is_pair in pairs:
        cnt = 2 if is_pair else 1
        lhs.append(refs[pos:pos + cnt])
        pos += cnt
    w_hbm, x_ref, gate_ref = refs[pos:pos + 3]
    pos += 3
    if final:
        gain_ref, o_ref = refs[pos:pos + 2]
        wb_ref, stage_ref, sem = refs[pos + 2:]
    else:
        sh_ref, sc_ref, xo_ref, h_ref = refs[pos:pos + 4]
        wb_ref, stage_ref, sem = refs[pos + 4:]
    k_total = sum(ks)
    m = pl.program_id(0)

    @pl.when(m == 0)
    def _():
        chunks = [(r, c) for r in range(k_total // kc) for c in range(D // wc)]

        def copy(i):
            r, c = chunks[i]
            return pltpu.make_async_copy(
                w_hbm.at[layer, pl.ds(r * kc, kc), pl.ds(c * wc, wc)], stage_ref.at[i % 2], sem.at[i % 2])

        copy(0).start()
        for i, (r, c) in enumerate(chunks):
            if i + 1 < len(chunks):
                copy(i + 1).start()
            copy(i).wait()
            wb_ref[r * kc:(r + 1) * kc, c * wc:(c + 1) * wc] = stage_ref[i % 2].astype(BF16)

    off = 0
    acc = None
    for part, k in zip(lhs, ks):
        a = part[0][...]
        if len(part) == 2:
            a = jnp.where(m < n_lat, a, part[1][...])
        prod = jnp.dot(a, wb_ref[off:off + k, :], preferred_element_type=F32)
        acc = prod if acc is None else acc + prod
        off += k
    xn = x_ref[...] + gate_ref[...] * acc
    if final:
        ms = jnp.mean(xn * xn, axis=-1, keepdims=True)
        o_ref[...] = xn * lax.rsqrt(ms + EPS) * gain_ref[...]
    else:
        xo_ref[...] = xn
        h_ref[...] = _modulate(xn, sh_ref[...], sc_ref[...]).astype(BF16)


def _resproj_call(lhs, w_all, layer, x, mod, mod_layer, gate_idx, next_layer, next_idx, final_gain, *,
                  n_rows, tm, kc, wc, name):
    n_lat = ML // tm
    pairs = tuple(isinstance(a, tuple) for a in lhs)
    ks = tuple((a[0] if p else a).shape[1] for a, p in zip(lhs, pairs))
    k_total = sum(ks)
    final = final_gain is not None
    in_specs, args = [], []
    for a, p, k in zip(lhs, pairs, ks):
        if p:
            in_specs += [pl.BlockSpec((tm, k), lambda m: (jnp.minimum(m, n_lat - 1), 0)),
                         pl.BlockSpec((tm, k), lambda m: (jnp.maximum(m - n_lat, 0), 0))]
            args += [a[0], a[1]]
        else:
            in_specs.append(pl.BlockSpec((tm, k), lambda m: (m, 0)))
            args.append(a)
    in_specs += [pl.BlockSpec(memory_space=pl.ANY),
                 pl.BlockSpec((tm, D), lambda m: (m, 0)),
                 _mod_spec(mod_layer, gate_idx, tm, 0)]
    args += [w_all, x, mod]
    if final:
        in_specs.append(pl.BlockSpec((1, D), lambda m: (0, 0)))
        args.append(final_gain.reshape(1, D))
        out_specs = pl.BlockSpec((tm, D), lambda m: (m, 0))
        out_shape = jax.ShapeDtypeStruct((n_rows, D), F32)
    else:
        in_specs += [_mod_spec(next_layer, next_idx[0], tm, 0), _mod_spec(next_layer, next_idx[1], tm, 0)]
        args += [mod, mod]
        out_specs = [pl.BlockSpec((tm, D), lambda m: (m, 0)), pl.BlockSpec((tm, D), lambda m: (m, 0))]
        out_shape = [jax.ShapeDtypeStruct((n_rows, D), F32), jax.ShapeDtypeStruct((n_rows, D), BF16)]
    return pl.pallas_call(
        functools.partial(_resproj_kernel, ks=ks, pairs=pairs, layer=layer, n_lat=n_lat, kc=kc, wc=wc,
                          final=final),
        grid=(n_rows // tm,),
        in_specs=in_specs,
        out_specs=out_specs,
        out_shape=out_shape,
        scratch_shapes=[pltpu.VMEM((k_total, D), BF16),
                        pltpu.VMEM((2, kc, wc), F32),
                        pltpu.SemaphoreType.DMA((2,))],
        compiler_params=_cparams(1),
        name=name,
    )(*args)


FFN_HALO = 16


def _ffn_up_kernel(x_ref, xp_ref, xn_ref, wg_ref, wu0_ref, wu1_ref, cw_ref, cb_ref, o_ref,
                   wgb_ref, wub_ref, xe_ref, *, tm, tn, u_skip):
    n = pl.program_id(0)
    m = pl.program_id(1)

    @pl.when(m == 0)
    def _():
        wgb_ref[...] = wg_ref[...].astype(BF16)
        wub_ref[:, :tn - u_skip] = wu0_ref[:, u_skip:].astype(BF16)
        wub_ref[:, tn - u_skip:] = wu1_ref[:, :u_skip].astype(BF16)

    hl = FFN_HALO
    xe_ref[0:hl, :] = xp_ref[...]
    xe_ref[hl:hl + tm, :] = x_ref[...]
    xe_ref[hl + tm:, :] = xn_ref[...]
    ge = jnp.dot(xe_ref[...], wgb_ref[...], preferred_element_type=F32)
    u = jnp.dot(x_ref[...], wub_ref[...], preferred_element_type=F32)
    rows = tm + 2 * hl
    g_prev = pltpu.roll(ge, 1, 0)[hl:hl + tm]
    g_next = pltpu.roll(ge, rows - 1, 0)[hl:hl + tm]
    g_mid = ge[hl:hl + tm]
    r = m * tm + lax.broadcasted_iota(jnp.int32, (tm, tn), 0)
    seq = jnp.where(r < ML, S, CL)
    t = jnp.bitwise_and(r, seq - 1)
    g_prev = jnp.where(t == 0, 0.0, g_prev)
    g_next = jnp.where(t == seq - 1, 0.0, g_next)
    conv = cw_ref[0:1, :] * g_prev + cw_ref[1:2, :] * g_mid + cw_ref[2:3, :] * g_next + cb_ref[...]
    val = conv * jax.nn.sigmoid(conv) * u
    col = n * tn + lax.broadcasted_iota(jnp.int32, (tm, tn), 1)
    o_ref[...] = jnp.where(col < D_FF, val, 0.0).astype(BF16)


def _ffn_up_call(h, w_all, layer, conv_w, conv_b, n_rows):
    tm, tn, hl = 1024, 512, FFN_HALO
    r = tm // hl
    n_halo_blocks = n_rows // hl
    u_blk0, u_skip = divmod(D_FF, tn)
    last_blk = pl.cdiv(2 * D_FF, tn) - 1
    return pl.pallas_call(
        functools.partial(_ffn_up_kernel, tm=tm, tn=tn, u_skip=u_skip),
        grid=(pl.cdiv(D_FF, tn), n_rows // tm),
        in_specs=[
            pl.BlockSpec((tm, D), lambda n, m: (m, 0)),
            pl.BlockSpec((hl, D), lambda n, m: (jnp.maximum(m * r - 1, 0), 0)),
            pl.BlockSpec((hl, D), lambda n, m: (jnp.minimum((m + 1) * r, n_halo_blocks - 1), 0)),
            pl.BlockSpec((None, D, tn), lambda n, m: (layer, 0, n)),
            pl.BlockSpec((None, D, tn), lambda n, m: (layer, 0, u_blk0 + n)),
            pl.BlockSpec((None, D, tn), lambda n, m: (layer, 0, jnp.minimum(u_blk0 + n + 1, last_blk))),
            pl.BlockSpec((FFN_K, tn), lambda n, m: (0, n)),
            pl.BlockSpec((1, tn), lambda n, m: (0, n)),
        ],
        out_specs=pl.BlockSpec((tm, tn), lambda n, m: (m, n)),
        out_shape=jax.ShapeDtypeStruct((n_rows, D_FF), BF16),
        scratch_shapes=[pltpu.VMEM((D, tn), BF16), pltpu.VMEM((D, tn), BF16),
                        pltpu.VMEM((tm + 2 * hl, D), BF16)],
        compiler_params=_cparams(2),
        name="ffn_up",
    )(h, h, h, w_all, w_all, w_all, conv_w, conv_b.reshape(1, D_FF))


def _axial_rope_tables(length, rot_dim):
    rows = length // GRID_W
    row = jnp.repeat(jnp.arange(rows, dtype=F32), GRID_W)
    col = jnp.tile(jnp.arange(GRID_W, dtype=F32), rows)
    n = rot_dim // 4
    inv = jnp.power(ROPE_THETA, -jnp.arange(n, dtype=F32) / n)
    ang = jnp.concatenate([row[:, None] * inv, col[:, None] * inv], axis=-1)
    return jnp.cos(ang), jnp.sin(ang)


def _rope_tables_gqa():
    c, s = _axial_rope_tables(S, HEAD_DIM)
    cos = jnp.concatenate([c, c], axis=1)
    sin = jnp.concatenate([-s, s], axis=1)
    ident_c = jnp.ones((MC, LANE), F32)
    ident_s = jnp.zeros((MC, LANE), F32)
    return jnp.concatenate([cos, ident_c], axis=0), jnp.concatenate([sin, ident_s], axis=0)


def _rope_tables_mla():
    c, s = _axial_rope_tables(S, MLA_ROPE)
    z = jnp.zeros_like(c)
    cos = jnp.concatenate([c, z, c, z], axis=1)
    sin = jnp.concatenate([-s, z, s, z], axis=1)
    one = jnp.ones((MC, 32), F32)
    zc = jnp.zeros((MC, 32), F32)
    ident_c = jnp.concatenate([one, zc, one, zc], axis=1)
    return jnp.concatenate([cos, ident_c], axis=0), jnp.concatenate([sin, jnp.zeros((MC, LANE), F32)], axis=0)


def _spread_rope_cols(w):
    h = MLA_ROPE // 2
    z = jnp.zeros(w.shape[:-1] + (h,), w.dtype)
    return jnp.concatenate([w[..., :h], z, w[..., h:], z], axis=-1)


def kernel(x, c, ctx, c_ctx, ada_w, ada_b, ev_w_in, ev_q_gain, ev_k_gain, ev_w_out, od_w_in, od_conv_w, od_conv_b, od_ln_g, od_ln_b, od_q_norm, od_w_uq, od_kv_norm, od_w_ukv, od_w_out, ffn_w_up, ffn_conv_w, ffn_conv_b, ffn_w_down, final_norm):
    cvec = jnp.concatenate([c, c_ctx[None, :], jnp.zeros((8 - NB - 1, D), F32)], axis=0)
    mod = _ada_call(cvec, ada_w, ada_b).reshape(DEPTH, 8, 6, 1, D)
    cos_a, sin_a = _rope_tables_gqa()
    cos_m, sin_m = _rope_tables_mla()

    od_wt_in = jnp.swapaxes(od_w_in, 1, 2)
    n_odd = od_w_in.shape[0]
    wq = od_w_uq.reshape(n_odd, Q_LORA, MLA_HEADS, MLA_NOPE + MLA_ROPE)
    wq = jnp.concatenate([wq[..., :MLA_NOPE], _spread_rope_cols(wq[..., MLA_NOPE:])], axis=-1)
    wq = wq.reshape(n_odd, Q_LORA, MLA_HEADS * MLA_QK)

    xl, h = _init_call(x.reshape(ML, D), ctx.reshape(MC, D), mod)

    for i in range(DEPTH):
        need_ctx = i < DEPTH - 1
        n_rows = MT if need_ctx else ML
        j = i // 2
        if i % 2 == 0:
            gain_full = jnp.concatenate([
                jnp.ones((FOURIER_W,), F32), jnp.tile(ev_q_gain[j], N_Q_HEADS),
                jnp.tile(ev_k_gain[j], N_KV_HEADS), jnp.ones((KV_W,), F32)]).reshape(1, EVEN_IN_W)
            z = _even_inproj_call(h, ev_w_in, j, gain_full, cos_a, sin_a)
            att = _gqa_calls(z, need_ctx)
            mix = _fourier_calls(z, Q_W // FOURIER_W, need_ctx)
            w_out = ev_w_out
        else:
            z = _odd_inproj_call(h, od_wt_in, j, cos_m, sin_m)
            q = _small_proj_call(z, 2, Q_LORA, od_q_norm[j].reshape(1, Q_LORA), wq, j, cos_m, sin_m,
                                 n_rows=n_rows, norm=True, pre_scale=QS_MLA, rope="odd", name="mla_q_up")
            kv = _small_proj_call(z, 3, KV_LORA, od_kv_norm[j].reshape(1, KV_LORA), od_w_ukv, j,
                                  cos_m, sin_m, n_rows=MT, norm=True, pre_scale=1.0, rope="none",
                                  name="mla_kv_up")
            att = _mla_calls(q, kv, z, (2 * CONV_W + Q_LORA + KV_LORA) // LANE, need_ctx)
            mix = _conv_call(z, od_conv_w[j], od_conv_b[j], od_ln_g[j], od_ln_b[j], n_rows)
            w_out = od_w_out

        def rows(v):
            if not isinstance(v, tuple):
                return v
            return v if v[1] is not None else v[0]

        xl, h = _resproj_call([rows(mix), rows(att)], w_out, j, xl, mod, i, 2, i, (3, 4), None,
                              n_rows=n_rows, tm=512, kc=256, wc=D, name="out_proj")
        hid = _ffn_up_call(h, ffn_w_up, i, ffn_conv_w[i], ffn_conv_b[i], n_rows)
        if need_ctx:
            xl, h = _resproj_call([hid], ffn_w_down, i, xl, mod, i, 5, i + 1, (0, 1), None,
                                  n_rows=n_rows, tm=256, kc=688, wc=D // 2, name="ffn_down")
        else:
            out = _resproj_call([hid], ffn_w_down, i, xl, mod, i, 5, None, None, final_norm,
                                n_rows=n_rows, tm=256, kc=688, wc=D // 2, name="ffn_down_final")
    return out.reshape(NB, S, D)
```

```python
import functools
import math

import jax
import jax.numpy as jnp
from jax import lax
from jax.experimental import pallas as pl
from jax.experimental.pallas import tpu as pltpu

F32 = jnp.float32
BF16 = jnp.bfloat16

D = 2048
NB = 4
S = 2048
DEPTH = 4
GRID_W = 64
CL = 256
ROPE_THETA = 10000.0
EPS = 1e-6

N_FG = 4
FG_W = 128
FOURIER_W = 512
HEAD_DIM = 128
N_Q_HEADS = 12
N_KV_HEADS = 4
GQA_G = N_Q_HEADS // N_KV_HEADS
Q_W = 1536
KV_W = 512
EVEN_IN_W = 3072

CONV_W = 512
CONV_K = 31
MLA_HEADS = 12
MLA_NOPE = 128
MLA_ROPE = 64
MLA_V = 128
Q_LORA = 512
KV_LORA = 512
MLA_QK = 256

D_FF = 5504
FFN_K = 3

ML = NB * S
MC = NB * CL
MT = ML + MC

LANE = 128
SUBLANES = 8
VMEM_LIMIT = 56 * 1024 * 1024

LOG2E = math.log2(math.e)
QS_GQA = HEAD_DIM ** -0.5 * LOG2E
QS_MLA = (MLA_NOPE + MLA_ROPE) ** -0.5 * LOG2E


def _cparams(n_axes):
    return pltpu.CompilerParams(
        dimension_semantics=("arbitrary",) * n_axes, vmem_limit_bytes=VMEM_LIMIT)


def _seg(m, tm):
    return jnp.minimum((m * tm) // S, NB)


def _mod_spec(layer, which, tm, m_axis):
    def idx(*g):
        return (layer, _seg(g[m_axis], tm), which, 0, 0)
    return pl.BlockSpec((None, None, None, 1, D), idx)


def _rope_idx(m, tm):
    per_seq = S // tm
    n_lat = ML // tm
    return jnp.where(m < n_lat, m % per_seq, per_seq + (m - n_lat))


def _modulate(x, sh, sc):
    ms = jnp.mean(x * x, axis=-1, keepdims=True)
    return x * lax.rsqrt(ms + EPS) * (1.0 + sc) + sh


def _rope128(y, cos, sin):
    return y * cos + pltpu.roll(y, 64, 1) * sin


def _ada_kernel(c_ref, w_ref, b_ref, o_ref):
    c = c_ref[...]
    s = (c * jax.nn.sigmoid(c)).astype(BF16)
    o_ref[...] = jnp.dot(s, w_ref[...].astype(BF16), preferred_element_type=F32) + b_ref[...]


def _ada_call(cvec, ada_w, ada_b):
    tn = 1024
    return pl.pallas_call(
        _ada_kernel,
        grid=(DEPTH, 6 * D // tn),
        in_specs=[
            pl.BlockSpec((8, D), lambda l, j: (0, 0)),
            pl.BlockSpec((None, D, tn), lambda l, j: (l, 0, j)),
            pl.BlockSpec((None, 1, tn), lambda l, j: (l, 0, j)),
        ],
        out_specs=pl.BlockSpec((None, 8, tn), lambda l, j: (l, 0, j)),
        out_shape=jax.ShapeDtypeStruct((DEPTH, 8, 6 * D), F32),
        compiler_params=_cparams(2),
        name="ada_ln",
    )(cvec, ada_w, ada_b.reshape(DEPTH, 1, 6 * D))


def _init_kernel(x_ref, c_ref, sh_ref, sc_ref, xo_ref, h_ref, *, n_lat):
    m = pl.program_id(0)

    @pl.when(m < n_lat)
    def _():
        xo_ref[...] = x_ref[...]

    @pl.when(m >= n_lat)
    def _():
        xo_ref[...] = c_ref[...]

    h_ref[...] = _modulate(xo_ref[...], sh_ref[...], sc_ref[...]).astype(BF16)


def _init_call(x2, c2, mod):
    tm = 512
    n_lat = ML // tm
    return pl.pallas_call(
        functools.partial(_init_kernel, n_lat=n_lat),
        grid=(MT // tm,),
        in_specs=[
            pl.BlockSpec((tm, D), lambda m: (jnp.minimum(m, n_lat - 1), 0)),
            pl.BlockSpec((tm, D), lambda m: (jnp.maximum(m - n_lat, 0), 0)),
            _mod_spec(0, 0, tm, 0),
            _mod_spec(0, 1, tm, 0),
        ],
        out_specs=[pl.BlockSpec((tm, D), lambda m: (m, 0)),
                   pl.BlockSpec((tm, D), lambda m: (m, 0))],
        out_shape=[jax.ShapeDtypeStruct((MT, D), F32), jax.ShapeDtypeStruct((MT, D), BF16)],
        compiler_params=_cparams(1),
        name="init_modulate",
    )(x2, c2, mod, mod)


def _cast_weight(w_ref, wb_ref):
    @pl.when(pl.program_id(1) == 0)
    def _():
        wb_ref[...] = w_ref[...].astype(BF16)


def _even_inproj_kernel(x_ref, w_ref, gain_ref, cos_ref, sin_ref, o_ref, wb_ref, *, tn):
    n = pl.program_id(0)
    _cast_weight(w_ref, wb_ref)
    acc = jnp.dot(x_ref[...], wb_ref[...], preferred_element_type=F32)
    plain = jnp.logical_or(n == 0, n == 5)

    @pl.when(plain)
    def _():
        o_ref[...] = acc.astype(BF16)

    @pl.when(jnp.logical_not(plain))
    def _():
        qs = jnp.where(n < 4, QS_GQA, 1.0)
        for j in range(tn // HEAD_DIM):
            sl = slice(j * HEAD_DIM, (j + 1) * HEAD_DIM)
            a = acc[:, sl]
            g = gain_ref[:, sl] * qs
            y = a * lax.rsqrt(jnp.mean(a * a, axis=-1, keepdims=True) + EPS) * g
            o_ref[:, sl] = _rope128(y, cos_ref[...], sin_ref[...]).astype(BF16)


def _even_inproj_call(h, w_all, layer, gain_full, cos_t, sin_t):
    tm, tn = 1024, 512
    def out_idx(n, m):
        return (m, jnp.where(n == 0, 3, jnp.where(n < 4, n - 1, n)))
    return pl.pallas_call(
        functools.partial(_even_inproj_kernel, tn=tn),
        grid=(EVEN_IN_W // tn, MT // tm),
        in_specs=[
            pl.BlockSpec((tm, D), lambda n, m: (m, 0)),
            pl.BlockSpec((None, D, tn), lambda n, m: (layer, 0, n)),
            pl.BlockSpec((1, tn), lambda n, m: (0, n)),
            pl.BlockSpec((tm, HEAD_DIM), lambda n, m: (_rope_idx(m, tm), 0)),
            pl.BlockSpec((tm, HEAD_DIM), lambda n, m: (_rope_idx(m, tm), 0)),
        ],
        out_specs=pl.BlockSpec((tm, tn), out_idx),
        out_shape=jax.ShapeDtypeStruct((MT, EVEN_IN_W), BF16),
        scratch_shapes=[pltpu.VMEM((D, tn), BF16)],
        compiler_params=_cparams(2),
        name="even_in_proj",
    )(h, w_all, gain_full, cos_t, sin_t)


def _odd_inproj_kernel(x_ref, wt_ref, cos_ref, sin_ref, o_ref, wb_ref, *, n_plain):
    n = pl.program_id(0)

    @pl.when(pl.program_id(1) == 0)
    def _():
        wb_ref[...] = wt_ref[...].T.astype(BF16)

    @pl.when(n < n_plain)
    def _():
        o_ref[...] = jnp.dot(x_ref[...], wb_ref[...], preferred_element_type=F32).astype(BF16)

    @pl.when(n == n_plain)
    def _():
        a = jnp.dot(x_ref[...], wb_ref[:, 0:LANE], preferred_element_type=F32)
        lane = lax.broadcasted_iota(jnp.int32, a.shape, 1)
        h = MLA_ROPE // 2
        upper = jnp.where(jnp.logical_and(lane >= 2 * h, lane < 3 * h), pltpu.roll(a, h, 1), 0.0)
        a = jnp.where(lane < h, a, upper)
        o_ref[:, 0:LANE] = _rope128(a, cos_ref[...], sin_ref[...]).astype(BF16)
        o_ref[:, LANE:] = jnp.zeros((o_ref.shape[0], o_ref.shape[1] - LANE), BF16)


ODD_Z_W = 2560


def _odd_inproj_call(h, wt_all, layer, cos_t, sin_t):
    tm, tn = 1024, 512
    n_plain = (2 * CONV_W + Q_LORA + KV_LORA) // tn
    return pl.pallas_call(
        functools.partial(_odd_inproj_kernel, n_plain=n_plain),
        grid=(n_plain + 1, MT // tm),
        in_specs=[
            pl.BlockSpec((tm, D), lambda n, m: (m, 0)),
            pl.BlockSpec((None, tn, D), lambda n, m: (layer, n, 0)),
            pl.BlockSpec((tm, LANE), lambda n, m: (_rope_idx(m, tm), 0)),
            pl.BlockSpec((tm, LANE), lambda n, m: (_rope_idx(m, tm), 0)),
        ],
        out_specs=pl.BlockSpec((tm, tn), lambda n, m: (m, n)),
        out_shape=jax.ShapeDtypeStruct((MT, ODD_Z_W), BF16),
        scratch_shapes=[pltpu.VMEM((D, tn), BF16)],
        compiler_params=_cparams(2),
        name="odd_in_proj",
    )(h, wt_all, cos_t, sin_t)


def _small_proj_kernel(x_ref, g_ref, w_ref, cos_ref, sin_ref, o_ref, wb_ref, *, norm, pre_scale, rope, cw):
    @pl.when(pl.program_id(0) == 0)
    def _():
        wb_ref[...] = w_ref[...].astype(BF16)

    x = x_ref[...]
    if norm:
        xf = x.astype(F32)
        g = g_ref[...] * pre_scale
        x = (xf * lax.rsqrt(jnp.mean(xf * xf, axis=-1, keepdims=True) + EPS) * g).astype(BF16)
    n_out = o_ref.shape[1]
    for j in range(n_out // cw):
        acc = jnp.dot(x, wb_ref[:, j * cw:(j + 1) * cw], preferred_element_type=F32)
        for i in range(cw // LANE):
            col = j * cw + i * LANE
            a = acc[:, i * LANE:(i + 1) * LANE]
            roped = rope == "all" or (rope == "odd" and (col // LANE) % 2 == 1)
            if roped:
                a = _rope128(a, cos_ref[...], sin_ref[...])
            o_ref[:, col:col + LANE] = a.astype(BF16)


def _small_proj_call(x, x_col_block, k_in, gain, w_all, layer, cos_t, sin_t, *, n_rows, norm,
                     pre_scale, rope, name):
    tm = 512
    n_out = w_all.shape[2]
    cw = min(512, n_out)
    return pl.pallas_call(
        functools.partial(_small_proj_kernel, norm=norm, pre_scale=pre_scale, rope=rope, cw=cw),
        grid=(n_rows // tm,),
        in_specs=[
            pl.BlockSpec((tm, k_in), lambda m: (m, x_col_block)),
            pl.BlockSpec((1, k_in), lambda m: (0, 0)),
            pl.BlockSpec((None, k_in, n_out), lambda m: (layer, 0, 0)),
            pl.BlockSpec((tm, LANE), lambda m: (_rope_idx(m, tm), 0)),
            pl.BlockSpec((tm, LANE), lambda m: (_rope_idx(m, tm), 0)),
        ],
        out_specs=pl.BlockSpec((tm, n_out), lambda m: (m, 0)),
        out_shape=jax.ShapeDtypeStruct((n_rows, n_out), BF16),
        scratch_shapes=[pltpu.VMEM((k_in, n_out), BF16)],
        compiler_params=_cparams(1),
        name=name,
    )(x, gain, w_all, cos_t, sin_t)


ATTN_KC = CL
ATTN_TQ = 1024


def _attend_heads(heads, n_chunks):
    qk = (((1,), (1,)), ((), ()))
    pv_dn = (((0,), (0,)), ((), ()))

    def fold(a, op):
        return op(a.reshape(a.shape[0] // SUBLANES, SUBLANES, a.shape[1]), axis=0)

    for q, key_chunk, val_chunk, write in heads:
        scores, mx = [], None
        for c in range(n_chunks):
            s = lax.dot_general(key_chunk(c), q, qk, preferred_element_type=F32)
            scores.append(s)
            part = fold(s, jnp.max)
            mx = part if mx is None else jnp.maximum(mx, part)
        m = jnp.max(mx, axis=0, keepdims=True)
        den = acc = None
        for c in range(n_chunks):
            p = jnp.exp2(scores[c] - m)
            part = fold(p, jnp.sum)
            den = part if den is None else den + part
            pv = lax.dot_general(val_chunk(c), p.astype(BF16), pv_dn, preferred_element_type=F32)
            acc = pv if acc is None else acc + pv
        write((acc / jnp.sum(den, axis=0, keepdims=True)).T)


def _chunk(lat_ref, ctx_ref, c, n_lat, cols):
    if c < n_lat:
        return lat_ref[c * ATTN_KC:(c + 1) * ATTN_KC, cols]
    return ctx_ref[:, cols]


def _gqa_kernel(*refs, has_lat, kvh):
    if has_lat:
        q_ref, kl_ref, kc_ref, vl_ref, vc_ref, o_ref = refs
    else:
        q_ref, kc_ref, vc_ref, o_ref = refs
        kl_ref = vl_ref = None
    n_lat = S // ATTN_KC if has_lat else 0
    heads = []
    for hk in range(kvh):
        kcols = slice(hk * HEAD_DIM, (hk + 1) * HEAD_DIM)
        for g in range(GQA_G):
            cols = slice((hk * GQA_G + g) * HEAD_DIM, (hk * GQA_G + g + 1) * HEAD_DIM)

            def write(o, cols=cols):
                o_ref[:, cols] = o.astype(BF16)

            heads.append((q_ref[:, cols],
                          lambda c, kcols=kcols: _chunk(kl_ref, kc_ref, c, n_lat, kcols),
                          lambda c, kcols=kcols: _chunk(vl_ref, vc_ref, c, n_lat, kcols), write))
    _attend_heads(heads, n_lat + 1)


def _gqa_calls(z, need_ctx):
    tq = ATTN_TQ
    k_blk0 = (Q_W + FOURIER_W) // HEAD_DIM
    v_blk0 = k_blk0 + N_KV_HEADS
    ctx_blk0 = ML // CL
    qw = GQA_G * HEAD_DIM
    lat = pl.pallas_call(
        functools.partial(_gqa_kernel, has_lat=True, kvh=1),
        grid=(NB, N_KV_HEADS, S // tq),
        in_specs=[
            pl.BlockSpec((tq, qw), lambda b, h, i: (b * (S // tq) + i, h)),
            pl.BlockSpec((S, HEAD_DIM), lambda b, h, i: (b, k_blk0 + h)),
            pl.BlockSpec((CL, HEAD_DIM), lambda b, h, i: (ctx_blk0 + b, k_blk0 + h)),
            pl.BlockSpec((S, HEAD_DIM), lambda b, h, i: (b, v_blk0 + h)),
            pl.BlockSpec((CL, HEAD_DIM), lambda b, h, i: (ctx_blk0 + b, v_blk0 + h)),
        ],
        out_specs=pl.BlockSpec((tq, qw), lambda b, h, i: (b * (S // tq) + i, h)),
        out_shape=jax.ShapeDtypeStruct((ML, Q_W), BF16),
        compiler_params=_cparams(3),
        name="gqa_lat",
    )(z, z, z, z, z)
    if not need_ctx:
        return lat, None
    ctx = pl.pallas_call(
        functools.partial(_gqa_kernel, has_lat=False, kvh=N_KV_HEADS),
        grid=(NB,),
        in_specs=[
            pl.BlockSpec((CL, Q_W), lambda b: (ctx_blk0 + b, 0)),
            pl.BlockSpec((CL, KV_W), lambda b: (ctx_blk0 + b, k_blk0 * HEAD_DIM // KV_W)),
            pl.BlockSpec((CL, KV_W), lambda b: (ctx_blk0 + b, v_blk0 * HEAD_DIM // KV_W)),
        ],
        out_specs=pl.BlockSpec((CL, Q_W), lambda b: (b, 0)),
        out_shape=jax.ShapeDtypeStruct((MC, Q_W), BF16),
        compiler_params=_cparams(1),
        name="gqa_ctx",
    )(z, z, z)
    return lat, ctx


MLA_HPB = 6


def _mla_kernel(*refs, has_lat, hpb):
    if has_lat:
        q_ref, kvl_ref, kvc_ref, krl_ref, krc_ref, o_ref = refs
    else:
        q_ref, kvc_ref, krc_ref, o_ref = refs
        kvl_ref = krl_ref = None
    n_lat = S // ATTN_KC if has_lat else 0
    full = slice(None)
    hw = MLA_NOPE + MLA_V
    heads = []
    for g in range(hpb):
        def key_chunk(c, g=g):
            nope = _chunk(kvl_ref, kvc_ref, c, n_lat, slice(g * hw, g * hw + MLA_NOPE))
            return jnp.concatenate([nope, _chunk(krl_ref, krc_ref, c, n_lat, full)], axis=1)

        def val_chunk(c, g=g):
            return _chunk(kvl_ref, kvc_ref, c, n_lat, slice(g * hw + MLA_NOPE, (g + 1) * hw))

        def write(o, g=g):
            o_ref[:, g * MLA_V:(g + 1) * MLA_V] = o.astype(BF16)

        heads.append((q_ref[:, g * MLA_QK:(g + 1) * MLA_QK], key_chunk, val_chunk, write))
    _attend_heads(heads, n_lat + 1)


def _mla_calls(q, kv, kr, kr_blk, need_ctx):
    tq = ATTN_TQ
    hb = MLA_HEADS // MLA_HPB
    qw, kvw, ow = MLA_HPB * MLA_QK, MLA_HPB * (MLA_NOPE + MLA_V), MLA_HPB * MLA_V
    ctx_blk0 = ML // CL
    lat = pl.pallas_call(
        functools.partial(_mla_kernel, has_lat=True, hpb=MLA_HPB),
        grid=(NB, hb, S // tq),
        in_specs=[
            pl.BlockSpec((tq, qw), lambda b, h, i: (b * (S // tq) + i, h)),
            pl.BlockSpec((S, kvw), lambda b, h, i: (b, h)),
            pl.BlockSpec((CL, kvw), lambda b, h, i: (ctx_blk0 + b, h)),
            pl.BlockSpec((S, LANE), lambda b, h, i: (b, kr_blk)),
            pl.BlockSpec((CL, LANE), lambda b, h, i: (ctx_blk0 + b, kr_blk)),
        ],
        out_specs=pl.BlockSpec((tq, ow), lambda b, h, i: (b * (S // tq) + i, h)),
        out_shape=jax.ShapeDtypeStruct((ML, MLA_HEADS * MLA_V), BF16),
        compiler_params=_cparams(3),
        name="mla_lat",
    )(q, kv, kv, kr, kr)
    if not need_ctx:
        return lat, None
    ctx = pl.pallas_call(
        functools.partial(_mla_kernel, has_lat=False, hpb=MLA_HEADS),
        grid=(NB,),
        in_specs=[
            pl.BlockSpec((CL, MLA_HEADS * MLA_QK), lambda b: (ctx_blk0 + b, 0)),
            pl.BlockSpec((CL, MLA_HEADS * (MLA_NOPE + MLA_V)), lambda b: (ctx_blk0 + b, 0)),
            pl.BlockSpec((CL, LANE), lambda b: (ctx_blk0 + b, kr_blk)),
        ],
        out_specs=pl.BlockSpec((CL, MLA_HEADS * MLA_V), lambda b: (b, 0)),
        out_shape=jax.ShapeDtypeStruct((MC, MLA_HEADS * MLA_V), BF16),
        compiler_params=_cparams(1),
        name="mla_ctx",
    )(q, kv, kr)
    return lat, ctx


def _fourier_kernel(x_ref, wc_ref, cl_ref, sl_ref, o_ref, ya_ref, yb_ref):
    @pl.when(pl.program_id(1) == 0)
    def _():
        for g in range(N_FG):
            sl = slice(g * FG_W, (g + 1) * FG_W)
            y = jnp.dot(x_ref[:, sl], wc_ref[...], preferred_element_type=F32)
            ya_ref[:, sl] = y[:, :FG_W].astype(BF16)
            yb_ref[:, sl] = y[:, FG_W:].astype(BF16)

    o = (jnp.dot(cl_ref[...], ya_ref[...], preferred_element_type=F32)
         - jnp.dot(sl_ref[...], yb_ref[...], preferred_element_type=F32))
    o_ref[...] = o.astype(BF16)


def _dft_mats(n):
    hi = 64 if n % 64 == 0 and n > 64 else 1
    l = jnp.arange(n, dtype=jnp.int32)[:, None]

    def trig(step, count):
        k = (l * (jnp.arange(count, dtype=jnp.int32)[None, :] * step)) % n
        ang = k.astype(F32) * (2.0 * jnp.pi / n)
        return jnp.cos(ang), jnp.sin(ang)

    nrm = float(n) ** -0.5
    if hi == 1:
        c, s = trig(1, n)
        return c * nrm, s * nrm
    ca, sa = trig(hi, n // hi)
    cb, sb = trig(1, hi)
    ca, sa = ca[:, :, None] * nrm, sa[:, :, None] * nrm
    cb, sb = cb[:, None, :], sb[:, None, :]
    return (ca * cb - sa * sb).reshape(n, n), (sa * cb + ca * sb).reshape(n, n)


def _fourier_calls(z, f_col_block, need_ctx):
    cc, sc = _dft_mats(FG_W)
    wc = jnp.concatenate([cc, sc], axis=1).astype(BF16)
    cl, sl = _dft_mats(S)
    cl, sl = cl.astype(BF16).T, sl.astype(BF16).T
    tr = 512
    lat = pl.pallas_call(
        _fourier_kernel,
        grid=(NB, S // tr),
        in_specs=[
            pl.BlockSpec((S, FOURIER_W), lambda b, r: (b, f_col_block)),
            pl.BlockSpec((FG_W, 2 * FG_W), lambda b, r: (0, 0)),
            pl.BlockSpec((tr, S), lambda b, r: (r, 0)),
            pl.BlockSpec((tr, S), lambda b, r: (r, 0)),
        ],
        out_specs=pl.BlockSpec((tr, FOURIER_W), lambda b, r: (b * (S // tr) + r, 0)),
        out_shape=jax.ShapeDtypeStruct((ML, FOURIER_W), BF16),
        scratch_shapes=[pltpu.VMEM((S, FOURIER_W), BF16), pltpu.VMEM((S, FOURIER_W), BF16)],
        compiler_params=_cparams(2),
        name="fourier_lat",
    )(z, wc, cl, sl)
    if not need_ctx:
        return lat, None
    clc, slc = _dft_mats(CL)
    clc, slc = clc.astype(BF16), slc.astype(BF16)
    ctx_blk0 = ML // CL
    ctx = pl.pallas_call(
        _fourier_kernel,
        grid=(NB, 1),
        in_specs=[
            pl.BlockSpec((CL, FOURIER_W), lambda b, r: (ctx_blk0 + b, f_col_block)),
            pl.BlockSpec((FG_W, 2 * FG_W), lambda b, r: (0, 0)),
            pl.BlockSpec((CL, CL), lambda b, r: (0, 0)),
            pl.BlockSpec((CL, CL), lambda b, r: (0, 0)),
        ],
        out_specs=pl.BlockSpec((CL, FOURIER_W), lambda b, r: (b, 0)),
        out_shape=jax.ShapeDtypeStruct((MC, FOURIER_W), BF16),
        scratch_shapes=[pltpu.VMEM((CL, FOURIER_W), BF16), pltpu.VMEM((CL, FOURIER_W), BF16)],
        compiler_params=_cparams(2),
        name="fourier_ctx",
    )(z, wc, clc, slc)
    return lat, ctx


CONV_TM = 256
CONV_HALO = 16
CONV_CHUNK = 32


def _conv_kernel(a_ref, g_ref, ap_ref, gp_ref, an_ref, gn_ref, w_ref, b_ref, lg_ref, lb_ref,
                 o_ref, ext_ref, *, n_lat, per_seq):
    m = pl.program_id(0)
    is_ctx = m >= n_lat
    first = jnp.logical_or(is_ctx, m % per_seq == 0)
    last = jnp.logical_or(is_ctx, m % per_seq == per_seq - 1)

    def glu(a, g):
        return a.astype(F32) * jax.nn.sigmoid(g.astype(F32))

    tm, hl, sl = CONV_TM, CONV_HALO, SUBLANES
    ext_ref[0, 0:hl, :] = jnp.where(first, 0.0, glu(ap_ref[...], gp_ref[...]))
    ext_ref[0, hl:hl + tm, :] = glu(a_ref[...], g_ref[...])
    ext_ref[0, hl + tm:, :] = jnp.where(last, 0.0, glu(an_ref[...], gn_ref[...]))
    span = tm + 2 * hl - sl
    for s in range(1, sl):
        ext_ref[s, 0:span, :] = ext_ref[0, s:s + span, :]

    pad = CONV_K // 2
    for c in range(tm // CONV_CHUNK):
        base = c * CONV_CHUNK + hl - pad
        acc = jnp.zeros((CONV_CHUNK, CONV_W), F32) + b_ref[...]
        for k in range(CONV_K):
            s, a = (base + k) % sl, (base + k) // sl * sl
            acc = acc + w_ref[k:k + 1, :] * ext_ref[s, a:a + CONV_CHUNK, :]
        mu = jnp.mean(acc, axis=-1, keepdims=True)
        d = acc - mu
        var = jnp.mean(d * d, axis=-1, keepdims=True)
        y = d * lax.rsqrt(var + EPS) * lg_ref[...] + lb_ref[...]
        o_ref[c * CONV_CHUNK:(c + 1) * CONV_CHUNK, :] = (y * jax.nn.sigmoid(y)).astype(BF16)


def _conv_call(z, conv_w, conv_b, ln_g, ln_b, n_rows):
    tm, hl = CONV_TM, CONV_HALO
    r = tm // hl
    n_halo_blocks = z.shape[0] // hl
    prev = lambda m: jnp.maximum(m * r - 1, 0)
    nxt = lambda m: jnp.minimum((m + 1) * r, n_halo_blocks - 1)
    row = lambda v: v.reshape(1, CONV_W)
    return pl.pallas_call(
        functools.partial(_conv_kernel, n_lat=ML // tm, per_seq=S // tm),
        grid=(n_rows // tm,),
        in_specs=[
            pl.BlockSpec((tm, CONV_W), lambda m: (m, 0)),
            pl.BlockSpec((tm, CONV_W), lambda m: (m, 1)),
            pl.BlockSpec((hl, CONV_W), lambda m: (prev(m), 0)),
            pl.BlockSpec((hl, CONV_W), lambda m: (prev(m), 1)),
            pl.BlockSpec((hl, CONV_W), lambda m: (nxt(m), 0)),
            pl.BlockSpec((hl, CONV_W), lambda m: (nxt(m), 1)),
            pl.BlockSpec((CONV_K, CONV_W), lambda m: (0, 0)),
            pl.BlockSpec((1, CONV_W), lambda m: (0, 0)),
            pl.BlockSpec((1, CONV_W), lambda m: (0, 0)),
            pl.BlockSpec((1, CONV_W), lambda m: (0, 0)),
        ],
        out_specs=pl.BlockSpec((tm, CONV_W), lambda m: (m, 0)),
        out_shape=jax.ShapeDtypeStruct((n_rows, CONV_W), BF16),
        scratch_shapes=[pltpu.VMEM((SUBLANES, tm + 2 * hl, CONV_W), F32)],
        compiler_params=_cparams(1),
        name="conformer_conv",
    )(z, z, z, z, z, z, conv_w, row(conv_b), row(ln_g), row(ln_b))


def _resproj_kernel(*refs, ks, pairs, layer, n_lat, kc, wc, final):
    pos = 0
    lhs = []
    for is_pair in pairs:
        cnt = 2 if is_pair else 1
        lhs.append(refs[pos:pos + cnt])
        pos += cnt
    w_hbm, x_ref, gate_ref = refs[pos:pos + 3]
    pos += 3
    if final:
        gain_ref, o_ref = refs[pos:pos + 2]
        wb_ref, stage_ref, sem = refs[pos + 2:]
    else:
        sh_ref, sc_ref, xo_ref, h_ref = refs[pos:pos + 4]
        wb_ref, stage_ref, sem = refs[pos + 4:]
    k_total = sum(ks)
    m = pl.program_id(0)

    @pl.when(m == 0)
    def _():
        chunks = [(r, c) for r in range(k_total // kc) for c in range(D // wc)]

        def copy(i):
            r, c = chunks[i]
            return pltpu.make_async_copy(
                w_hbm.at[layer, pl.ds(r * kc, kc), pl.ds(c * wc, wc)], stage_ref.at[i % 2], sem.at[i % 2])

        copy(0).start()
        for i, (r, c) in enumerate(chunks):
            if i + 1 < len(chunks):
                copy(i + 1).start()
            copy(i).wait()
            wb_ref[r * kc:(r + 1) * kc, c * wc:(c + 1) * wc] = stage_ref[i % 2].astype(BF16)

    off = 0
    acc = None
    for part, k in zip(lhs, ks):
        a = part[0][...]
        if len(part) == 2:
            a = jnp.where(m < n_lat, a, part[1][...])
        prod = jnp.dot(a, wb_ref[off:off + k, :], preferred_element_type=F32)
        acc = prod if acc is None else acc + prod
        off += k
    xn = x_ref[...] + gate_ref[...] * acc
    if final:
        ms = jnp.mean(xn * xn, axis=-1, keepdims=True)
        o_ref[...] = xn * lax.rsqrt(ms + EPS) * gain_ref[...]
    else:
        xo_ref[...] = xn
        h_ref[...] = _modulate(xn, sh_ref[...], sc_ref[...]).astype(BF16)


def _resproj_call(lhs, w_all, layer, x, mod, mod_layer, gate_idx, next_layer, next_idx, final_gain, *,
                  n_rows, tm, kc, wc, name):
    n_lat = ML // tm
    pairs = tuple(isinstance(a, tuple) for a in lhs)
    ks = tuple((a[0] if p else a).shape[1] for a, p in zip(lhs, pairs))
    k_total = sum(ks)
    final = final_gain is not None
    in_specs, args = [], []
    for a, p, k in zip(lhs, pairs, ks):
        if p:
            in_specs += [pl.BlockSpec((tm, k), lambda m: (jnp.minimum(m, n_lat - 1), 0)),
                         pl.BlockSpec((tm, k), lambda m: (jnp.maximum(m - n_lat, 0), 0))]
            args += [a[0], a[1]]
        else:
            in_specs.append(pl.BlockSpec((tm, k), lambda m: (m, 0)))
            args.append(a)
    in_specs += [pl.BlockSpec(memory_space=pl.ANY),
                 pl.BlockSpec((tm, D), lambda m: (m, 0)),
                 _mod_spec(mod_layer, gate_idx, tm, 0)]
    args += [w_all, x, mod]
    if final:
        in_specs.append(pl.BlockSpec((1, D), lambda m: (0, 0)))
        args.append(final_gain.reshape(1, D))
        out_specs = pl.BlockSpec((tm, D), lambda m: (m, 0))
        out_shape = jax.ShapeDtypeStruct((n_rows, D), F32)
    else:
        in_specs += [_mod_spec(next_layer, next_idx[0], tm, 0), _mod_spec(next_layer, next_idx[1], tm, 0)]
        args += [mod, mod]
        out_specs = [pl.BlockSpec((tm, D), lambda m: (m, 0)), pl.BlockSpec((tm, D), lambda m: (m, 0))]
        out_shape = [jax.ShapeDtypeStruct((n_rows, D), F32), jax.ShapeDtypeStruct((n_rows, D), BF16)]
    return pl.pallas_call(
        functools.partial(_resproj_kernel, ks=ks, pairs=pairs, layer=layer, n_lat=n_lat, kc=kc, wc=wc,
                          final=final),
        grid=(n_rows // tm,),
        in_specs=in_specs,
        out_specs=out_specs,
        out_shape=out_shape,
        scratch_shapes=[pltpu.VMEM((k_total, D), BF16),
                        pltpu.VMEM((2, kc, wc), F32),
                        pltpu.SemaphoreType.DMA((2,))],
        compiler_params=_cparams(1),
        name=name,
    )(*args)


FFN_HALO = 16


def _ffn_up_kernel(xw_ref, wg_ref, wu0_ref, wu1_ref, cw_ref, cb_ref, o_ref,
                   wgb_ref, wub_ref, *, tm, tn, u_skip):
    n = pl.program_id(0)
    m = pl.program_id(1)

    @pl.when(m == 0)
    def _():
        wgb_ref[...] = wg_ref[...].astype(BF16)
        wub_ref[:, :tn - u_skip] = wu0_ref[:, u_skip:].astype(BF16)
        wub_ref[:, tn - u_skip:] = wu1_ref[:, :u_skip].astype(BF16)

    def tile(off):
        ge = jnp.dot(xw_ref[...], wgb_ref[...], preferred_element_type=F32)
        u = jnp.dot(xw_ref[off:off + tm, :], wub_ref[...], preferred_element_type=F32)
        rows = tm + 2 * FFN_HALO
        g_prev = pltpu.roll(ge, 1, 0)[off:off + tm]
        g_next = pltpu.roll(ge, rows - 1, 0)[off:off + tm]
        g_mid = ge[off:off + tm]
        r = m * tm + lax.broadcasted_iota(jnp.int32, (tm, tn), 0)
        seq = jnp.where(r < ML, S, CL)
        t = jnp.bitwise_and(r, seq - 1)
        g_prev = jnp.where(t == 0, 0.0, g_prev)
        g_next = jnp.where(t == seq - 1, 0.0, g_next)
        conv = cw_ref[0:1, :] * g_prev + cw_ref[1:2, :] * g_mid + cw_ref[2:3, :] * g_next + cb_ref[...]
        val = conv * jax.nn.sigmoid(conv) * u
        col = n * tn + lax.broadcasted_iota(jnp.int32, (tm, tn), 1)
        o_ref[...] = jnp.where(col < D_FF, val, 0.0).astype(BF16)

    @pl.when(m == 0)
    def _():
        tile(0)

    @pl.when(m > 0)
    def _():
        tile(FFN_HALO)


def _ffn_up_call(h, w_all, layer, conv_w, conv_b, n_rows):
    tm, tn, hl = 1024, 512, FFN_HALO
    u_blk0, u_skip = divmod(D_FF, tn)
    last_blk = pl.cdiv(2 * D_FF, tn) - 1
    return pl.pallas_call(
        functools.partial(_ffn_up_kernel, tm=tm, tn=tn, u_skip=u_skip),
        grid=(pl.cdiv(D_FF, tn), n_rows // tm),
        in_specs=[
            pl.BlockSpec((pl.Element(tm + 2 * hl, (0, 2 * hl)), pl.Element(D)),
                         lambda n, m: ((m * (tm // hl) - 1) * jnp.minimum(m, 1) * hl, 0)),
            pl.BlockSpec((None, D, tn), lambda n, m: (layer, 0, n)),
            pl.BlockSpec((None, D, tn), lambda n, m: (layer, 0, u_blk0 + n)),
            pl.BlockSpec((None, D, tn), lambda n, m: (layer, 0, jnp.minimum(u_blk0 + n + 1, last_blk))),
            pl.BlockSpec((FFN_K, tn), lambda n, m: (0, n)),
            pl.BlockSpec((1, tn), lambda n, m: (0, n)),
        ],
        out_specs=pl.BlockSpec((tm, tn), lambda n, m: (m, n)),
        out_shape=jax.ShapeDtypeStruct((n_rows, D_FF), BF16),
        scratch_shapes=[pltpu.VMEM((D, tn), BF16), pltpu.VMEM((D, tn), BF16)],
        compiler_params=_cparams(2),
        name="ffn_up",
    )(h, w_all, w_all, w_all, conv_w, conv_b.reshape(1, D_FF))


def _axial_rope_tables(length, rot_dim):
    rows = length // GRID_W
    row = jnp.repeat(jnp.arange(rows, dtype=F32), GRID_W)
    col = jnp.tile(jnp.arange(GRID_W, dtype=F32), rows)
    n = rot_dim // 4
    inv = jnp.power(ROPE_THETA, -jnp.arange(n, dtype=F32) / n)
    ang = jnp.concatenate([row[:, None] * inv, col[:, None] * inv], axis=-1)
    return jnp.cos(ang), jnp.sin(ang)


def _rope_tables_gqa():
    c, s = _axial_rope_tables(S, HEAD_DIM)
    cos = jnp.concatenate([c, c], axis=1)
    sin = jnp.concatenate([-s, s], axis=1)
    ident_c = jnp.ones((MC, LANE), F32)
    ident_s = jnp.zeros((MC, LANE), F32)
    return jnp.concatenate([cos, ident_c], axis=0), jnp.concatenate([sin, ident_s], axis=0)


def _rope_tables_mla():
    c, s = _axial_rope_tables(S, MLA_ROPE)
    z = jnp.zeros_like(c)
    cos = jnp.concatenate([c, z, c, z], axis=1)
    sin = jnp.concatenate([-s, z, s, z], axis=1)
    one = jnp.ones((MC, 32), F32)
    zc = jnp.zeros((MC, 32), F32)
    ident_c = jnp.concatenate([one, zc, one, zc], axis=1)
    return jnp.concatenate([cos, ident_c], axis=0), jnp.concatenate([sin, jnp.zeros((MC, LANE), F32)], axis=0)


def _spread_rope_cols(w):
    h = MLA_ROPE // 2
    z = jnp.zeros(w.shape[:-1] + (h,), w.dtype)
    return jnp.concatenate([w[..., :h], z, w[..., h:], z], axis=-1)


def kernel(x, c, ctx, c_ctx, ada_w, ada_b, ev_w_in, ev_q_gain, ev_k_gain, ev_w_out, od_w_in, od_conv_w, od_conv_b, od_ln_g, od_ln_b, od_q_norm, od_w_uq, od_kv_norm, od_w_ukv, od_w_out, ffn_w_up, ffn_conv_w, ffn_conv_b, ffn_w_down, final_norm):
    cvec = jnp.concatenate([c, c_ctx[None, :], jnp.zeros((8 - NB - 1, D), F32)], axis=0)
    mod = _ada_call(cvec, ada_w, ada_b).reshape(DEPTH, 8, 6, 1, D)
    cos_a, sin_a = _rope_tables_gqa()
    cos_m, sin_m = _rope_tables_mla()

    od_wt_in = jnp.swapaxes(od_w_in, 1, 2)
    n_odd = od_w_in.shape[0]
    wq = od_w_uq.reshape(n_odd, Q_LORA, MLA_HEADS, MLA_NOPE + MLA_ROPE)
    wq = jnp.concatenate([wq[..., :MLA_NOPE], _spread_rope_cols(wq[..., MLA_NOPE:])], axis=-1)
    wq = wq.reshape(n_odd, Q_LORA, MLA_HEADS * MLA_QK)

    xl, h = _init_call(x.reshape(ML, D), ctx.reshape(MC, D), mod)

    for i in range(DEPTH):
        need_ctx = i < DEPTH - 1
        n_rows = MT if need_ctx else ML
        j = i // 2
        if i % 2 == 0:
            gain_full = jnp.concatenate([
                jnp.ones((FOURIER_W,), F32), jnp.tile(ev_q_gain[j], N_Q_HEADS),
                jnp.tile(ev_k_gain[j], N_KV_HEADS), jnp.ones((KV_W,), F32)]).reshape(1, EVEN_IN_W)
            z = _even_inproj_call(h, ev_w_in, j, gain_full, cos_a, sin_a)
            att = _gqa_calls(z, need_ctx)
            mix = _fourier_calls(z, Q_W // FOURIER_W, need_ctx)
            w_out = ev_w_out
        else:
            z = _odd_inproj_call(h, od_wt_in, j, cos_m, sin_m)
            q = _small_proj_call(z, 2, Q_LORA, od_q_norm[j].reshape(1, Q_LORA), wq, j, cos_m, sin_m,
                                 n_rows=n_rows, norm=True, pre_scale=QS_MLA, rope="odd", name="mla_q_up")
            kv = _small_proj_call(z, 3, KV_LORA, od_kv_norm[j].reshape(1, KV_LORA), od_w_ukv, j,
                                  cos_m, sin_m, n_rows=MT, norm=True, pre_scale=1.0, rope="none",
                                  name="mla_kv_up")
            att = _mla_calls(q, kv, z, (2 * CONV_W + Q_LORA + KV_LORA) // LANE, need_ctx)
            mix = _conv_call(z, od_conv_w[j], od_conv_b[j], od_ln_g[j], od_ln_b[j], n_rows)
            w_out = od_w_out

        def rows(v):
            if not isinstance(v, tuple):
                return v
            return v if v[1] is not None else v[0]

        xl, h = _resproj_call([rows(mix), rows(att)], w_out, j, xl, mod, i, 2, i, (3, 4), None,
                              n_rows=n_rows, tm=512, kc=256, wc=D, name="out_proj")
        hid = _ffn_up_call(h, ffn_w_up, i, ffn_conv_w[i], ffn_conv_b[i], n_rows)
        if need_ctx:
            xl, h = _resproj_call([hid], ffn_w_down, i, xl, mod, i, 5, i + 1, (0, 1), None,
                                  n_rows=n_rows, tm=256, kc=688, wc=D // 2, name="ffn_down")
        else:
            out = _resproj_call([hid], ffn_w_down, i, xl, mod, i, 5, None, None, final_norm,
                                n_rows=n_rows, tm=256, kc=688, wc=D // 2, name="ffn_down_final")
    return out.reshape(NB, S, D)
```
